```python
import math
import jax, jax.numpy as jnp
from jax import lax
import numpy as np

D_MODEL = 2048
BATCH = 4
SEQ = 2048
DEPTH = 4

D_MIX = D_MODEL
D_CONV = D_MIX // 2
D_MLSTM = D_MIX - D_CONV
N_MLSTM_HEADS = 4
MLSTM_HEAD_DIM = D_MLSTM // N_MLSTM_HEADS
CONV_WIDTH = 31
QK_CONV_WIDTH = 4
CHUNK = 128
D_IN = 2 * D_CONV + 4 * D_MLSTM + 2 * N_MLSTM_HEADS
SPLITS = [D_CONV, 2 * D_CONV, 2 * D_CONV + 2 * D_MLSTM, 2 * D_CONV + 3 * D_MLSTM,
          2 * D_CONV + 4 * D_MLSTM, 2 * D_CONV + 4 * D_MLSTM + N_MLSTM_HEADS]
MEM_LEN = 256
N_XHEADS = 4
XHEAD_DIM = D_MODEL // N_XHEADS
D_FF = 256 * ((8 * D_MODEL // 3 + 255) // 256)
N_EXPERTS = 8
TOP_K = 2
N_DENSE = (DEPTH + 1) // 2
N_MOE = DEPTH // 2
EPS = 1e-6

kernel_name = "hybrid_conv_mlstm_moe_decoder"


def rmsnorm(x, g):
    xf = x.astype(jnp.float32)
    y = xf * lax.rsqrt(jnp.mean(xf * xf, axis=-1, keepdims=True) + EPS)
    return (y * g.astype(jnp.float32)).astype(x.dtype)


def layernorm(x, g, b):
    xf = x.astype(jnp.float32)
    mu = jnp.mean(xf, axis=-1, keepdims=True)
    var = jnp.mean(jnp.square(xf - mu), axis=-1, keepdims=True)
    y = (xf - mu) * lax.rsqrt(var + EPS) * g.astype(jnp.float32) + b.astype(jnp.float32)
    return y.astype(x.dtype)


def causal_dwconv(x, w, b):
    K, C = w.shape
    y = lax.conv_general_dilated(
        x, w[:, None, :].astype(x.dtype), window_strides=(1,), padding=[(K - 1, 0)],
        dimension_numbers=("NWC", "WIO", "NWC"), feature_group_count=C)
    return y + b.astype(x.dtype)


def conformer_conv(a, gate, w_dw, b_dw, ln_g, ln_b):
    u = a * jax.nn.sigmoid(gate)
    u = causal_dwconv(u, w_dw, b_dw)
    u = layernorm(u, ln_g, ln_b)
    return jax.nn.silu(u)


def mlstm_chunkwise(q, k, v, i_pre, f_pre):
    B, S, H, Dh = q.shape
    nc = S // CHUNK

    def to_chunks(t):
        return t.reshape(B, nc, CHUNK, H, -1).transpose(1, 0, 3, 2, 4)

    def gate_chunks(t):
        return t.reshape(B, nc, CHUNK, H).transpose(1, 0, 3, 2)

    qc, kc, vc = to_chunks(q), to_chunks(k), to_chunks(v)
    lic = gate_chunks(i_pre)
    bc = jnp.cumsum(gate_chunks(jax.nn.log_sigmoid(f_pre)), axis=-1)
    gc = bc[..., -1]
    mask = jnp.tril(jnp.ones((CHUNK, CHUNK), dtype=bool))

    def step(carry, xs):
        C, n, m = carry
        q_c, k_c, v_c, li_c, b_c, g_c = xs
        a = b_c + m[..., None]
        logw = jnp.where(mask, b_c[..., :, None] - b_c[..., None, :] + li_c[..., None, :], -jnp.inf)
        m_q = jnp.maximum(a, jnp.max(logw, axis=-1))
        w = jnp.exp(logw - m_q[..., None])
        inter = jnp.exp(a - m_q)
        s = jnp.einsum("bhld,bhsd->bhls", q_c, k_c) * w
        num = inter[..., None] * jnp.einsum("bhld,bhde->bhle", q_c, C) + jnp.einsum("bhls,bhse->bhle", s, v_c)
        den = inter * jnp.einsum("bhld,bhd->bhl", q_c, n) + jnp.sum(s, axis=-1)
        h = num / jnp.maximum(jnp.abs(den), jnp.exp(-m_q))[..., None]
        logu = g_c[..., None] - b_c + li_c
        m_new = jnp.maximum(g_c + m, jnp.max(logu, axis=-1))
        decay = jnp.exp(g_c + m - m_new)
        ku = k_c * jnp.exp(logu - m_new[..., None])[..., None]
        C_new = decay[..., None, None] * C + jnp.einsum("bhsd,bhse->bhde", ku, v_c)
        n_new = decay[..., None] * n + jnp.sum(ku, axis=2)
        return (C_new, n_new, m_new), h

    init = (jnp.zeros((B, H, Dh, Dh), jnp.float32), jnp.zeros((B, H, Dh), jnp.float32),
            jnp.zeros((B, H), jnp.float32))
    _, hs = lax.scan(step, init, (qc, kc, vc, lic, bc, gc))
    return hs.transpose(1, 0, 3, 2, 4).reshape(B, S, H, Dh)


def head_norm(x, g):
    mu = jnp.mean(x, axis=-1, keepdims=True)
    var = jnp.mean(jnp.square(x - mu), axis=-1, keepdims=True)
    return (x - mu) * lax.rsqrt(var + EPS) * g.astype(jnp.float32).reshape(N_MLSTM_HEADS, MLSTM_HEAD_DIM)


def parallel_mixer(h, w_in, conv_dw_w, conv_dw_b, conv_ln_g, conv_ln_b, qk_conv_w, qk_conv_b,
                   b_igate, b_fgate, mlstm_norm_g, w_out):
    B, S, _ = h.shape
    z = h @ w_in
    a, gt, qk, v, o, ig, fg = jnp.split(z, SPLITS, axis=-1)
    y_conv = conformer_conv(a, gt, conv_dw_w, conv_dw_b, conv_ln_g, conv_ln_b)
    qk = jax.nn.silu(causal_dwconv(qk, qk_conv_w, qk_conv_b)).astype(jnp.float32)
    q, k = jnp.split(qk, 2, axis=-1)
    hd = (B, S, N_MLSTM_HEADS, MLSTM_HEAD_DIM)
    q = q.reshape(hd)
    k = k.reshape(hd) * (MLSTM_HEAD_DIM ** -0.5)
    v = v.astype(jnp.float32).reshape(hd)
    i_pre = ig.astype(jnp.float32) + b_igate.astype(jnp.float32)
    f_pre = fg.astype(jnp.float32) + b_fgate.astype(jnp.float32)
    ht = mlstm_chunkwise(q, k, v, i_pre, f_pre)
    ht = jax.nn.sigmoid(o.astype(jnp.float32)).reshape(hd) * ht
    y_mlstm = head_norm(ht, mlstm_norm_g).reshape(B, S, D_MLSTM).astype(h.dtype)
    return jnp.concatenate([y_conv, y_mlstm], axis=-1) @ w_out


def memory_cross_attn(h, mem_n, wq, wk, wv, wo):
    B, S, _ = h.shape
    M = mem_n.shape[1]
    q = (h @ wq).reshape(B, S, N_XHEADS, XHEAD_DIM)
    k = (mem_n @ wk).reshape(B, M, N_XHEADS, XHEAD_DIM)
    v = (mem_n @ wv).reshape(B, M, N_XHEADS, XHEAD_DIM)
    sc = jnp.einsum("bshd,bmhd->bhsm", q, k).astype(jnp.float32) * (XHEAD_DIM ** -0.5)
    p = jax.nn.softmax(sc, axis=-1).astype(h.dtype)
    out = jnp.einsum("bhsm,bmhd->bshd", p, v).reshape(B, S, D_MODEL)
    return out @ wo


def swiglu(t, wg, wu, wd):
    return (jax.nn.silu(t @ wg) * (t @ wu)) @ wd


def moe_swiglu(h, w_r, b_r, wg, wu, wd):
    B, S, D = h.shape
    t = h.reshape(B * S, D)
    logits = (t @ w_r).astype(jnp.float32) + b_r.astype(jnp.float32)
    top_v, top_i = lax.top_k(logits, TOP_K)
    top_w = jax.nn.softmax(top_v, axis=-1)
    gate = jnp.sum(jax.nn.one_hot(top_i, N_EXPERTS, dtype=jnp.float32) * top_w[..., None], axis=1)
    gate = gate.astype(h.dtype)
    y = jnp.zeros_like(t)
    for e in range(N_EXPERTS):
        y = y + gate[:, e:e + 1] * swiglu(t, wg[e], wu[e], wd[e])
    return y.reshape(B, S, D)


def setup_inputs(seed: int = 0) -> dict:
    key = jax.random.key(seed)
    ks = iter(jax.random.split(key, 40))

    def nrm(shape, scale):
        return jax.random.normal(next(ks), shape, jnp.float32) * scale

    def gain(shape):
        return 1.0 + nrm(shape, 0.02)

    L = DEPTH
    H = N_MLSTM_HEADS
    return {
        "x": nrm((BATCH, SEQ, D_MODEL), 1.0),
        "mem": nrm((BATCH, MEM_LEN, D_MODEL), 1.0),
        "norm_mix": gain((L, D_MODEL)),
        "w_in": nrm((L, D_MODEL, D_IN), D_MODEL ** -0.5),
        "conv_dw_w": nrm((L, CONV_WIDTH, D_CONV), CONV_WIDTH ** -0.5),
        "conv_dw_b": nrm((L, D_CONV), 0.02),
        "conv_ln_g": gain((L, D_CONV)),
        "conv_ln_b": nrm((L, D_CONV), 0.02),
        "qk_conv_w": nrm((L, QK_CONV_WIDTH, 2 * D_MLSTM), QK_CONV_WIDTH ** -0.5),
        "qk_conv_b": nrm((L, 2 * D_MLSTM), 0.02),
        "b_igate": nrm((L, H), 0.1),
        "b_fgate": jnp.linspace(3.0, 6.0, H, dtype=jnp.float32)[None, :] + nrm((L, H), 0.1),
        "mlstm_norm_g": gain((L, D_MLSTM)),
        "w_out": nrm((L, D_MIX, D_MODEL), D_MIX ** -0.5),
        "norm_cross": gain((L, D_MODEL)),
        "norm_mem": gain((L, D_MODEL)),
        "w_cq": nrm((L, D_MODEL, D_MODEL), D_MODEL ** -0.5),
        "w_ck": nrm((L, D_MODEL, D_MODEL), D_MODEL ** -0.5),
        "w_cv": nrm((L, D_MODEL, D_MODEL), D_MODEL ** -0.5),
        "w_co": nrm((L, D_MODEL, D_MODEL), D_MODEL ** -0.5),
        "norm_ffn": gain((L, D_MODEL)),
        "w_gate_dense": nrm((N_DENSE, D_MODEL, D_FF), D_MODEL ** -0.5),
        "w_up_dense": nrm((N_DENSE, D_MODEL, D_FF), D_MODEL ** -0.5),
        "w_down_dense": nrm((N_DENSE, D_FF, D_MODEL), D_FF ** -0.5),
        "w_router": nrm((N_MOE, D_MODEL, N_EXPERTS), D_MODEL ** -0.5),
        "b_router": nrm((N_MOE, N_EXPERTS), 0.01),
        "w_gate_moe": nrm((N_MOE, N_EXPERTS, D_MODEL, D_FF), D_MODEL ** -0.5),
        "w_up_moe": nrm((N_MOE, N_EXPERTS, D_MODEL, D_FF), D_MODEL ** -0.5),
        "w_down_moe": nrm((N_MOE, N_EXPERTS, D_FF, D_MODEL), D_FF ** -0.5),
        "norm_final": gain((D_MODEL,)),
    }


def reference(x, mem, norm_mix, w_in, conv_dw_w, conv_dw_b, conv_ln_g, conv_ln_b, qk_conv_w, qk_conv_b,
              b_igate, b_fgate, mlstm_norm_g, w_out, norm_cross, norm_mem, w_cq, w_ck, w_cv, w_co,
              norm_ffn, w_gate_dense, w_up_dense, w_down_dense, w_router, b_router,
              w_gate_moe, w_up_moe, w_down_moe, norm_final):
    for l in range(DEPTH):
        h = rmsnorm(x, norm_mix[l])
        x = x + parallel_mixer(h, w_in[l], conv_dw_w[l], conv_dw_b[l], conv_ln_g[l], conv_ln_b[l],
                               qk_conv_w[l], qk_conv_b[l], b_igate[l], b_fgate[l], mlstm_norm_g[l], w_out[l])
        h = rmsnorm(x, norm_cross[l])
        mem_n = rmsnorm(mem, norm_mem[l])
        x = x + memory_cross_attn(h, mem_n, w_cq[l], w_ck[l], w_cv[l], w_co[l])
        h = rmsnorm(x, norm_ffn[l])
        j = l // 2
        if l % 2 == 0:
            x = x + swiglu(h, w_gate_dense[j], w_up_dense[j], w_down_dense[j])
        else:
            x = x + moe_swiglu(h, w_router[j], b_router[j], w_gate_moe[j], w_up_moe[j], w_down_moe[j])
    return rmsnorm(x, norm_final)
```

```python
import functools

import jax
import jax.numpy as jnp
from jax import lax
from jax.experimental import pallas as pl
from jax.experimental.pallas import tpu as pltpu

F32 = jnp.float32
BF16 = jnp.bfloat16
EPS = 1e-6

V7X_VMEM_BYTES = 64 * 1024 * 1024
VMEM_LIMIT = V7X_VMEM_BYTES - 8 * 1024 * 1024
LANES = 128
SUBLANES = 8

N_MLSTM_HEADS = 4
N_XHEADS = 4
CONV_WIDTH = 31
QK_CONV_WIDTH = 4
CHUNK = 128
N_EXPERTS = 8
MOE_TILE = 512
GATHER_ROWS = 256


def _params(*sem):
    return pltpu.CompilerParams(dimension_semantics=sem, vmem_limit_bytes=VMEM_LIMIT)


def _sigmoid(x):
    return 1.0 / (1.0 + jnp.exp(-x))


def _log_sigmoid(x):
    return jnp.minimum(x, 0.0) - jnp.log(1.0 + jnp.exp(-jnp.abs(x)))


def _rmsnorm_kernel(x_ref, g_ref, o_ref):
    x = x_ref[...]
    ms = jnp.mean(x * x, axis=-1, keepdims=True)
    o_ref[...] = (x * lax.rsqrt(ms + EPS) * g_ref[...]).astype(o_ref.dtype)


def _rmsnorm(x, g, out_dtype, tm=512):
    m, d = x.shape
    tm = min(tm, m)
    return pl.pallas_call(
        _rmsnorm_kernel,
        grid=(m // tm,),
        in_specs=[pl.BlockSpec((tm, d), lambda i: (i, 0)),
                  pl.BlockSpec((1, d), lambda i: (0, 0))],
        out_specs=pl.BlockSpec((tm, d), lambda i: (i, 0)),
        out_shape=jax.ShapeDtypeStruct((m, d), out_dtype),
        compiler_params=_params("arbitrary"),
        name="rmsnorm",
    )(x, g.reshape(1, d))


def _matmul_kernel(*refs, k_sizes, has_res):
    na = len(k_sizes)
    a_refs = refs[:na]
    w_ref = refs[na]
    r_ref = refs[na + 1] if has_res else None
    o_ref = refs[-1]
    acc = None
    k0 = 0
    for a_ref, ks in zip(a_refs, k_sizes):
        part = jnp.dot(a_ref[...], w_ref[k0:k0 + ks, :].astype(BF16),
                       preferred_element_type=F32)
        acc = part if acc is None else acc + part
        k0 += ks
    if has_res:
        acc = acc + r_ref[...]
    o_ref[...] = acc.astype(o_ref.dtype)


def _matmul(a_list, w, w_lead=(), *, n_cols=None, residual=None, out_dtype=F32,
            tm=1024, tn=512, name="matmul"):
    m = a_list[0].shape[0]
    k_sizes = tuple(a.shape[1] for a in a_list)
    k = sum(k_sizes)
    n = w.shape[-1] if n_cols is None else n_cols
    tm = min(tm, m)
    tn = min(tn, n)
    nlead = len(w_lead)
    w_block = (None,) * nlead + (k, tn)
    in_specs = [pl.BlockSpec((tm, ks), lambda j, i: (i, 0)) for ks in k_sizes]
    in_specs.append(pl.BlockSpec(w_block, lambda j, i: tuple(w_lead) + (0, j)))
    args = list(a_list) + [w]
    if residual is not None:
        in_specs.append(pl.BlockSpec((tm, tn), lambda j, i: (i, j)))
        args.append(residual)
    return pl.pallas_call(
        functools.partial(_matmul_kernel, k_sizes=k_sizes, has_res=residual is not None),
        grid=(n // tn, m // tm),
        in_specs=in_specs,
        out_specs=pl.BlockSpec((tm, tn), lambda j, i: (i, j)),
        out_shape=jax.ShapeDtypeStruct((m, n), out_dtype),
        compiler_params=_params("arbitrary", "arbitrary"),
        name=name,
    )(*args)


CONV_HALO = 32
CONV_ROWS = 32


def _conformer_kernel(a_ref, g_ref, w_ref, b_ref, lng_ref, lnb_ref, o_ref, ubuf, *, ts):
    s = pl.program_id(1)
    c = a_ref.shape[1]

    @pl.when(s == 0)
    def _():
        ubuf[0:CONV_HALO, :] = jnp.zeros((CONV_HALO, c), F32)

    @pl.when(s > 0)
    def _():
        ubuf[0:CONV_HALO, :] = ubuf[ts:ts + CONV_HALO, :]

    ubuf[CONV_HALO:CONV_HALO + ts, :] = a_ref[...] * _sigmoid(g_ref[...])
    bias = b_ref[...]
    lng = lng_ref[...]
    lnb = lnb_ref[...]
    for r in range(ts // CONV_ROWS):
        base = r * CONV_ROWS
        acc = jnp.broadcast_to(bias, (CONV_ROWS, c))
        for k in range(CONV_WIDTH):
            off = base + CONV_HALO - (CONV_WIDTH - 1) + k
            acc = acc + w_ref[k:k + 1, :] * ubuf[off:off + CONV_ROWS, :]
        mu = jnp.mean(acc, axis=-1, keepdims=True)
        d = acc - mu
        var = jnp.mean(d * d, axis=-1, keepdims=True)
        y = d * lax.rsqrt(var + EPS) * lng + lnb
        o_ref[base:base + CONV_ROWS, :] = (y * _sigmoid(y)).astype(o_ref.dtype)


def _conformer(z, w_dw, b_dw, ln_g, ln_b, batch, seq, ts=256):
    m = z.shape[0]
    c = w_dw.shape[1]
    ts = min(ts, seq)
    nt = seq // ts
    row = lambda b, s: (b * nt + s, 0)
    vec = pl.BlockSpec((1, c), lambda b, s: (0, 0))
    return pl.pallas_call(
        functools.partial(_conformer_kernel, ts=ts),
        grid=(batch, nt),
        in_specs=[pl.BlockSpec((ts, c), row),
                  pl.BlockSpec((ts, c), lambda b, s: (b * nt + s, 1)),
                  pl.BlockSpec((CONV_WIDTH, c), lambda b, s: (0, 0)),
                  vec, vec, vec],
        out_specs=pl.BlockSpec((ts, c), row),
        out_shape=jax.ShapeDtypeStruct((m, c), BF16),
        scratch_shapes=[pltpu.VMEM((CONV_HALO + ts, c), F32)],
        compiler_params=_params("arbitrary", "arbitrary"),
        name="conformer_conv",
    )(z, z, w_dw, b_dw.reshape(1, c), ln_g.reshape(1, c), ln_b.reshape(1, c))


QK_HALO = 8


def _mlstm_kernel(q_ref, k_ref, v_ref, o_ref, zg_ref, zgt_ref, cwq_ref, cwk_ref, cbq_ref, cbk_ref,
                  bcol_ref, brow_ref, ng_ref, out_ref, qbuf, kbuf, c_scr, n_scr, m_scr, *, nh, dh):
    ci = pl.program_id(1)
    L = q_ref.shape[0]
    w = q_ref.shape[1]

    @pl.when(ci == 0)
    def _():
        qbuf[0:QK_HALO, :] = jnp.zeros((QK_HALO, w), F32)
        kbuf[0:QK_HALO, :] = jnp.zeros((QK_HALO, w), F32)
        c_scr[...] = jnp.zeros(c_scr.shape, F32)
        n_scr[...] = jnp.zeros(n_scr.shape, F32)
        m_scr[...] = jnp.zeros(m_scr.shape, F32)

    @pl.when(ci > 0)
    def _():
        qbuf[0:QK_HALO, :] = qbuf[L:L + QK_HALO, :]
        kbuf[0:QK_HALO, :] = kbuf[L:L + QK_HALO, :]

    qbuf[QK_HALO:QK_HALO + L, :] = q_ref[...]
    kbuf[QK_HALO:QK_HALO + L, :] = k_ref[...]

    def short_conv(buf, cw_ref, cb_ref, lo, hi):
        acc = jnp.broadcast_to(cb_ref[:, lo:hi], (L, hi - lo))
        for j in range(QK_CONV_WIDTH):
            off = QK_HALO - (QK_CONV_WIDTH - 1) + j
            acc = acc + cw_ref[j:j + 1, lo:hi] * buf[off:off + L, lo:hi]
        return acc * _sigmoid(acc)

    zg = zg_ref[...] + bcol_ref[...]
    zgt = zgt_ref[...] + brow_ref[...]
    lf_c = _log_sigmoid(zg)
    lf_r = _log_sigmoid(zgt)
    row_i = lax.broadcasted_iota(jnp.int32, (L, L), 0)
    col_i = lax.broadcasted_iota(jnp.int32, (L, L), 1)
    causal = row_i >= col_i
    scale = dh ** -0.5
    nt_dims = (((1,), (1,)), ((), ()))
    tn_dims = (((0,), (0,)), ((), ()))

    for h in range(nh):
        lo, hi = h * dh, (h + 1) * dh
        li_col = zg[:, h:h + 1]
        lf_col = lf_c[:, nh + h:nh + h + 1]
        li_row = zgt[h:h + 1, :]
        lf_row = lf_r[nh + h:nh + h + 1, :]
        b_col = jnp.sum(jnp.where(causal, lf_row, 0.0), axis=1, keepdims=True)
        b_row = jnp.sum(jnp.where(row_i <= col_i, lf_col, 0.0), axis=0, keepdims=True)
        g = jnp.sum(lf_row, axis=1, keepdims=True)
        m_prev = m_scr[h:h + 1, 0:1]

        qh = short_conv(qbuf, cwq_ref, cbq_ref, lo, hi)
        kh = short_conv(kbuf, cwk_ref, cbk_ref, lo, hi) * scale
        vh = v_ref[:, lo:hi]
        qb = qh.astype(BF16)
        kb = kh.astype(BF16)
        vb = vh.astype(BF16)

        a_col = b_col + m_prev
        logw = jnp.where(causal, b_col - b_row + li_row, -jnp.inf)
        m_q = jnp.maximum(a_col, jnp.max(logw, axis=1, keepdims=True))
        wgt = jnp.exp(logw - m_q)
        inter = jnp.exp(a_col - m_q)
        s = lax.dot_general(qb, kb, nt_dims, preferred_element_type=F32) * wgt
        c_prev = c_scr[h]
        n_prev = n_scr[h:h + 1, :]
        num = (inter * jnp.dot(qb, c_prev.astype(BF16), preferred_element_type=F32)
               + jnp.dot(s.astype(BF16), vb, preferred_element_type=F32))
        den = (inter * jnp.sum(qh * n_prev, axis=1, keepdims=True)
               + jnp.sum(s, axis=1, keepdims=True))
        hout = num / jnp.maximum(jnp.abs(den), jnp.exp(-m_q))

        logu = g - b_col + li_col
        m_new = jnp.maximum(g + m_prev, jnp.max(logu, axis=0, keepdims=True))
        decay = jnp.exp(g + m_prev - m_new)
        ku = kh * jnp.exp(logu - m_new)
        c_scr[h] = decay * c_prev + lax.dot_general(ku.astype(BF16), vb, tn_dims,
                                                    preferred_element_type=F32)
        n_scr[h:h + 1, :] = decay * n_prev + jnp.sum(ku, axis=0, keepdims=True)
        m_scr[h:h + 1, :] = jnp.broadcast_to(m_new, (1, m_scr.shape[1]))

        ht = _sigmoid(o_ref[:, lo:hi]) * hout
        mu = jnp.mean(ht, axis=-1, keepdims=True)
        d = ht - mu
        var = jnp.mean(d * d, axis=-1, keepdims=True)
        out_ref[:, lo:hi] = (d * lax.rsqrt(var + EPS) * ng_ref[:, lo:hi]).astype(out_ref.dtype)


def _mlstm(z, zg, zgt, qk_w, qk_b, b_i, b_f, norm_g, batch, seq):
    m = z.shape[0]
    nh = N_MLSTM_HEADS
    w = norm_g.shape[0]
    dh = w // nh
    L = CHUNK
    nc = seq // L
    blk = lambda col: pl.BlockSpec((L, w), lambda b, c: (b * nc + c, col))
    full = lambda shape: pl.BlockSpec(shape, lambda b, c: (0,) * len(shape))
    bias = jnp.concatenate([b_i, b_f]).astype(F32)
    bcol = jnp.zeros((1, LANES), F32).at[0, :2 * nh].set(bias)
    brow = jnp.broadcast_to(bias[:, None], (2 * nh, L))
    return pl.pallas_call(
        functools.partial(_mlstm_kernel, nh=nh, dh=dh),
        grid=(batch, nc),
        in_specs=[blk(2), blk(3), blk(4), blk(5),
                  pl.BlockSpec((L, LANES), lambda b, c: (b * nc + c, 0)),
                  pl.BlockSpec((2 * nh, L), lambda b, c: (0, b * nc + c)),
                  full((QK_CONV_WIDTH, w)), full((QK_CONV_WIDTH, w)),
                  full((1, w)), full((1, w)),
                  full((1, LANES)), full((2 * nh, L)), full((1, w))],
        out_specs=pl.BlockSpec((L, w), lambda b, c: (b * nc + c, 0)),
        out_shape=jax.ShapeDtypeStruct((m, w), BF16),
        scratch_shapes=[pltpu.VMEM((QK_HALO + L, w), F32),
                        pltpu.VMEM((QK_HALO + L, w), F32),
                        pltpu.VMEM((nh, dh, dh), F32),
                        pltpu.VMEM((SUBLANES, dh), F32),
                        pltpu.VMEM((SUBLANES, LANES), F32)],
        compiler_params=_params("arbitrary", "arbitrary"),
        name="mlstm",
    )(z, z, z, z, zg, zgt, qk_w[:, :w], qk_w[:, w:], qk_b[:w].reshape(1, w), qk_b[w:].reshape(1, w),
      bcol, brow, norm_g.reshape(1, w))


def _xattn_kernel(q_ref, k_ref, v_ref, o_ref, *, nh):
    d = q_ref.shape[1]
    dh = d // nh
    scale = dh ** -0.5
    nt_dims = (((1,), (1,)), ((), ()))
    for h in range(nh):
        lo, hi = h * dh, (h + 1) * dh
        sc = lax.dot_general(q_ref[:, lo:hi], k_ref[:, lo:hi], nt_dims,
                             preferred_element_type=F32) * scale
        mx = jnp.max(sc, axis=-1, keepdims=True)
        e = jnp.exp(sc - mx)
        p = e / jnp.sum(e, axis=-1, keepdims=True)
        o_ref[:, lo:hi] = jnp.dot(p.astype(BF16), v_ref[:, lo:hi],
                                  preferred_element_type=F32).astype(o_ref.dtype)


def _xattn(q, k, v, batch, seq, mem_len, ts=512):
    m, d = q.shape
    ts = min(ts, seq)
    nt = seq // ts
    return pl.pallas_call(
        functools.partial(_xattn_kernel, nh=N_XHEADS),
        grid=(batch, nt),
        in_specs=[pl.BlockSpec((ts, d), lambda b, s: (b * nt + s, 0)),
                  pl.BlockSpec((mem_len, d), lambda b, s: (b, 0)),
                  pl.BlockSpec((mem_len, d), lambda b, s: (b, 0))],
        out_specs=pl.BlockSpec((ts, d), lambda b, s: (b * nt + s, 0)),
        out_shape=jax.ShapeDtypeStruct((m, d), BF16),
        compiler_params=_params("arbitrary", "arbitrary"),
        name="cross_attn",
    )(q, k, v)


def _gateup_kernel(h_ref, wg_ref, wu_ref, o_ref):
    h = h_ref[...]
    g = jnp.dot(h, wg_ref[...].astype(BF16), preferred_element_type=F32)
    u = jnp.dot(h, wu_ref[...].astype(BF16), preferred_element_type=F32)
    o_ref[...] = (g * _sigmoid(g) * u).astype(o_ref.dtype)


def _gateup(h, wg, wu, w_lead, tm=1024, tc=512):
    m, d = h.shape
    f = wg.shape[-1]
    tm = min(tm, m)
    nlead = len(w_lead)
    wspec = pl.BlockSpec((None,) * nlead + (d, tc), lambda c, i: tuple(w_lead) + (0, c))
    return pl.pallas_call(
        _gateup_kernel,
        grid=(f // tc, m // tm),
        in_specs=[pl.BlockSpec((tm, d), lambda c, i: (i, 0)), wspec, wspec],
        out_specs=pl.BlockSpec((tm, tc), lambda c, i: (i, c)),
        out_shape=jax.ShapeDtypeStruct((m, f), BF16),
        compiler_params=_params("arbitrary", "arbitrary"),
        name="swiglu_gateup",
    )(h, wg, wu)


def _moe_gateup_kernel(te_ref, ta_ref, h_ref, wg_ref, wu_ref, o_ref):
    i = pl.program_id(1)

    @pl.when(ta_ref[i] > 0)
    def _():
        _gateup_kernel(h_ref, wg_ref, wu_ref, o_ref)

    @pl.when(ta_ref[i] == 0)
    def _():
        o_ref[...] = jnp.zeros(o_ref.shape, o_ref.dtype)


def _moe_gateup(tile_e, tile_on, hs, wg, wu, layer, tc=512):
    r, d = hs.shape
    f = wg.shape[-1]
    tg = MOE_TILE
    wspec = pl.BlockSpec((None, None, d, tc), lambda c, i, te, ta: (layer, te[i], 0, c))
    return pl.pallas_call(
        _moe_gateup_kernel,
        grid_spec=pltpu.PrefetchScalarGridSpec(
            num_scalar_prefetch=2,
            grid=(f // tc, r // tg),
            in_specs=[pl.BlockSpec((tg, d), lambda c, i, te, ta: (i, 0)), wspec, wspec],
            out_specs=pl.BlockSpec((tg, tc), lambda c, i, te, ta: (i, c))),
        out_shape=jax.ShapeDtypeStruct((r, f), BF16),
        compiler_params=_params("arbitrary", "arbitrary"),
        name="moe_gateup",
    )(tile_e, tile_on, hs, wg, wu)


def _moe_down_kernel(te_ref, ta_ref, a_ref, w_ref, o_ref):
    i = pl.program_id(1)

    @pl.when(ta_ref[i] > 0)
    def _():
        o_ref[...] = jnp.dot(a_ref[...], w_ref[...].astype(BF16), preferred_element_type=F32)

    @pl.when(ta_ref[i] == 0)
    def _():
        o_ref[...] = jnp.zeros(o_ref.shape, o_ref.dtype)


def _moe_down(tile_e, tile_on, a, wd, layer, tn=256):
    r, f = a.shape
    d = wd.shape[-1]
    tg = MOE_TILE
    return pl.pallas_call(
        _moe_down_kernel,
        grid_spec=pltpu.PrefetchScalarGridSpec(
            num_scalar_prefetch=2,
            grid=(d // tn, r // tg),
            in_specs=[pl.BlockSpec((tg, f), lambda j, i, te, ta: (i, 0)),
                      pl.BlockSpec((None, None, f, tn), lambda j, i, te, ta: (layer, te[i], 0, j))],
            out_specs=pl.BlockSpec((tg, tn), lambda j, i, te, ta: (i, j))),
        out_shape=jax.ShapeDtypeStruct((r, d), F32),
        compiler_params=_params("arbitrary", "arbitrary"),
        name="moe_down",
    )(tile_e, tile_on, a, wd)


def _router_kernel(h_ref, wr_ref, br_ref, o_ref, *, ne):
    logits = jnp.dot(h_ref[...], wr_ref[...].astype(BF16), preferred_element_type=F32) + br_ref[...]
    lane = lax.broadcasted_iota(jnp.int32, logits.shape, 1).astype(F32)
    lg = jnp.where(lane < ne, logits, -jnp.inf)
    v1 = jnp.max(lg, axis=1, keepdims=True)
    i1 = jnp.min(jnp.where(lg == v1, lane, float(LANES)), axis=1, keepdims=True)
    lg2 = jnp.where(lane == i1, -jnp.inf, lg)
    v2 = jnp.max(lg2, axis=1, keepdims=True)
    i2 = jnp.min(jnp.where(lg2 == v2, lane, float(LANES)), axis=1, keepdims=True)
    e2 = jnp.exp(v2 - v1)
    w1 = 1.0 / (1.0 + e2)
    w2 = e2 / (1.0 + e2)
    o_ref[...] = jnp.where(lane == 0, i1, jnp.where(lane == 1, i2,
                           jnp.where(lane == 2, w1, jnp.where(lane == 3, w2, 0.0))))


def _router(h, w_r, b_r, tm=1024):
    m, d = h.shape
    ne = w_r.shape[1]
    tm = min(tm, m)
    wr = jnp.zeros((d, LANES), F32).at[:, :ne].set(w_r)
    br = jnp.zeros((1, LANES), F32).at[0, :ne].set(b_r)
    return pl.pallas_call(
        functools.partial(_router_kernel, ne=ne),
        grid=(m // tm,),
        in_specs=[pl.BlockSpec((tm, d), lambda i: (i, 0)),
                  pl.BlockSpec((d, LANES), lambda i: (0, 0)),
                  pl.BlockSpec((1, LANES), lambda i: (0, 0))],
        out_specs=pl.BlockSpec((tm, LANES), lambda i: (i, 0)),
        out_shape=jax.ShapeDtypeStruct((m, LANES), F32),
        compiler_params=_params("arbitrary"),
        name="moe_router",
    )(h, wr, br)


def _row_copy(src_hbm, row, buf, slot, sem):
    return pltpu.make_async_copy(src_hbm.at[pl.ds(row, 1), :], buf.at[pl.ds(slot, 1), :], sem)


def _dispatch_kernel(src_ref, x_hbm, g_ref, o_ref, buf, sem, *, rows):
    base = pl.program_id(0) * rows

    def start(r, carry):
        _row_copy(x_hbm, src_ref[base + r], buf, r, sem).start()
        return carry

    def wait(r, carry):
        _row_copy(x_hbm, 0, buf, r, sem).wait()
        return carry

    lax.fori_loop(0, rows, start, 0)
    lax.fori_loop(0, rows, wait, 0)
    x = buf[...]
    ms = jnp.mean(x * x, axis=-1, keepdims=True)
    o_ref[...] = (x * lax.rsqrt(ms + EPS) * g_ref[...]).astype(o_ref.dtype)


def _dispatch(src, x, g, n_rows):
    m, d = x.shape
    rows = GATHER_ROWS
    return pl.pallas_call(
        functools.partial(_dispatch_kernel, rows=rows),
        grid_spec=pltpu.PrefetchScalarGridSpec(
            num_scalar_prefetch=1,
            grid=(n_rows // rows,),
            in_specs=[pl.BlockSpec(memory_space=pl.ANY),
                      pl.BlockSpec((1, d), lambda i, s: (0, 0))],
            out_specs=pl.BlockSpec((rows, d), lambda i, s: (i, 0)),
            scratch_shapes=[pltpu.VMEM((rows, d), F32), pltpu.SemaphoreType.DMA(())]),
        out_shape=jax.ShapeDtypeStruct((n_rows, d), BF16),
        compiler_params=_params("arbitrary"),
        name="moe_dispatch",
    )(src, x, g.reshape(1, d))


def _combine_kernel(p1_ref, p2_ref, x_ref, route_ref, y_hbm, o_ref, buf1, buf2, sem, *, rows):
    base = pl.program_id(0) * rows

    def start(r, carry):
        _row_copy(y_hbm, p1_ref[base + r], buf1, r, sem.at[0]).start()
        _row_copy(y_hbm, p2_ref[base + r], buf2, r, sem.at[1]).start()
        return carry

    def wait(r, carry):
        _row_copy(y_hbm, 0, buf1, r, sem.at[0]).wait()
        _row_copy(y_hbm, 0, buf2, r, sem.at[1]).wait()
        return carry

    lax.fori_loop(0, rows, start, 0)
    lax.fori_loop(0, rows, wait, 0)
    route = route_ref[...]
    o_ref[...] = x_ref[...] + route[:, 2:3] * buf1[...] + route[:, 3:4] * buf2[...]


def _combine(p1, p2, x, route, y):
    m, d = x.shape
    rows = GATHER_ROWS
    return pl.pallas_call(
        functools.partial(_combine_kernel, rows=rows),
        grid_spec=pltpu.PrefetchScalarGridSpec(
            num_scalar_prefetch=2,
            grid=(m // rows,),
            in_specs=[pl.BlockSpec((rows, d), lambda i, a, b: (i, 0)),
                      pl.BlockSpec((rows, LANES), lambda i, a, b: (i, 0)),
                      pl.BlockSpec(memory_space=pl.ANY)],
            out_specs=pl.BlockSpec((rows, d), lambda i, a, b: (i, 0)),
            scratch_shapes=[pltpu.VMEM((rows, d), F32), pltpu.VMEM((rows, d), F32),
                            pltpu.SemaphoreType.DMA((2,))]),
        out_shape=jax.ShapeDtypeStruct((m, d), F32),
        compiler_params=_params("arbitrary"),
        name="moe_combine",
    )(p1, p2, x, route, y)


def _moe_plan(route, m, ne, tg):
    i1 = route[:, 0].astype(jnp.int32)
    i2 = route[:, 1].astype(jnp.int32)
    experts = jnp.arange(ne, dtype=jnp.int32)
    sel = ((i1[:, None] == experts) | (i2[:, None] == experts)).astype(jnp.int32)
    counts = jnp.sum(sel, axis=0)
    padded = ((counts + tg - 1) // tg) * tg
    ends = jnp.cumsum(padded)
    pos = (ends - padded)[None, :] + jnp.cumsum(sel, axis=0) - sel
    p1 = jnp.take_along_axis(pos, i1[:, None], axis=1)[:, 0]
    p2 = jnp.take_along_axis(pos, i2[:, None], axis=1)[:, 0]
    n_tiles = (2 * m) // tg + ne
    tok = jnp.arange(m, dtype=jnp.int32)
    src = jnp.zeros((n_tiles * tg,), jnp.int32).at[p1].set(tok).at[p2].set(tok)
    tile_start = jnp.arange(n_tiles, dtype=jnp.int32) * tg
    tile_e = jnp.minimum(jnp.sum((tile_start[:, None] >= ends[None, :]).astype(jnp.int32), axis=1), ne - 1)
    tile_on = (tile_start < ends[-1]).astype(jnp.int32)
    return p1, p2, src, tile_e, tile_on, n_tiles * tg


def _moe_block(x, g_ffn, w_r, b_r, wg, wu, wd, layer):
    m, d = x.shape
    h = _rmsnorm(x, g_ffn, BF16)
    route = _router(h, w_r, b_r)
    p1, p2, src, tile_e, tile_on, n_rows = _moe_plan(route, m, N_EXPERTS, MOE_TILE)
    hs = _dispatch(src, x, g_ffn, n_rows)
    a = _moe_gateup(tile_e, tile_on, hs, wg, wu, layer)
    y = _moe_down(tile_e, tile_on, a, wd, layer)
    return _combine(p1, p2, x, route, y)


def kernel(x, mem, norm_mix, w_in, conv_dw_w, conv_dw_b, conv_ln_g, conv_ln_b, qk_conv_w, qk_conv_b, b_igate, b_fgate, mlstm_norm_g, w_out, norm_cross, norm_mem, w_cq, w_ck, w_cv, w_co, norm_ffn, w_gate_dense, w_up_dense, w_down_dense, w_router, b_router, w_gate_moe, w_up_moe, w_down_moe, norm_final):
    batch, seq, d = x.shape
    mem_len = mem.shape[1]
    depth = norm_mix.shape[0]
    d_conv = conv_dw_w.shape[2]
    d_mlstm = mlstm_norm_g.shape[1]
    n_main = 2 * d_conv + 4 * d_mlstm
    nh = N_MLSTM_HEADS
    xf = x.reshape(batch * seq, d)
    memf = mem.reshape(batch * mem_len, d)

    for l in range(depth):
        h = _rmsnorm(xf, norm_mix[l], BF16)
        z = _matmul([h], w_in, (l,), n_cols=n_main, name="w_in")
        w_gates = jnp.zeros((d, LANES), F32).at[:, :2 * nh].set(w_in[l, :, n_main:])
        zg = _matmul([h], w_gates, tn=LANES, name="w_in_gates")
        zgt = zg[:, :2 * nh].T
        yc = _conformer(z, conv_dw_w[l], conv_dw_b[l], conv_ln_g[l], conv_ln_b[l], batch, seq)
        ym = _mlstm(z, zg, zgt, qk_conv_w[l], qk_conv_b[l], b_igate[l], b_fgate[l], mlstm_norm_g[l],
                    batch, seq)
        xf = _matmul([yc, ym], w_out, (l,), residual=xf, name="w_out")
        h = _rmsnorm(xf, norm_cross[l], BF16)
        mem_n = _rmsnorm(memf, norm_mem[l], BF16)
        q = _matmul([h], w_cq, (l,), out_dtype=BF16, name="w_cq")
        kk = _matmul([mem_n], w_ck, (l,), out_dtype=BF16, name="w_ck")
        vv = _matmul([mem_n], w_cv, (l,), out_dtype=BF16, name="w_cv")
        att = _xattn(q, kk, vv, batch, seq, mem_len)
        xf = _matmul([att], w_co, (l,), residual=xf, name="w_co")
        j = l // 2
        if l % 2 == 0:
            h = _rmsnorm(xf, norm_ffn[l], BF16)
            a = _gateup(h, w_gate_dense, w_up_dense, (j,))
            xf = _matmul([a], w_down_dense, (j,), residual=xf, tn=256, name="w_down")
        else:
            xf = _moe_block(xf, norm_ffn[l], w_router[j], b_router[j],
                            w_gate_moe, w_up_moe, w_down_moe, j)
    return _rmsnorm(xf, norm_final, F32).reshape(batch, seq, d)
```

```python
import functools

import jax
import jax.numpy as jnp
from jax import lax
from jax.experimental import pallas as pl
from jax.experimental.pallas import tpu as pltpu

F32 = jnp.float32
BF16 = jnp.bfloat16
EPS = 1e-6

V7X_VMEM_BYTES = 64 * 1024 * 1024
VMEM_LIMIT = V7X_VMEM_BYTES - 8 * 1024 * 1024
LANES = 128
SUBLANES = 8

N_MLSTM_HEADS = 4
N_XHEADS = 4
CONV_WIDTH = 31
QK_CONV_WIDTH = 4
CHUNK = 128
N_EXPERTS = 8
MOE_TILE = 512
GATHER_ROWS = 512
DMA_UNROLL = 8


def _params(*sem):
    return pltpu.CompilerParams(dimension_semantics=sem, vmem_limit_bytes=VMEM_LIMIT)


def _sigmoid(x):
    return 1.0 / (1.0 + jnp.exp(-x))


def _log_sigmoid(x):
    return jnp.minimum(x, 0.0) - jnp.log(1.0 + jnp.exp(-jnp.abs(x)))


def _rmsnorm_kernel(x_ref, g_ref, o_ref):
    x = x_ref[...]
    ms = jnp.mean(x * x, axis=-1, keepdims=True)
    o_ref[...] = (x * lax.rsqrt(ms + EPS) * g_ref[...]).astype(o_ref.dtype)


def _rmsnorm(x, g, out_dtype, tm=512):
    m, d = x.shape
    tm = min(tm, m)
    return pl.pallas_call(
        _rmsnorm_kernel,
        grid=(m // tm,),
        in_specs=[pl.BlockSpec((tm, d), lambda i: (i, 0)),
                  pl.BlockSpec((1, d), lambda i: (0, 0))],
        out_specs=pl.BlockSpec((tm, d), lambda i: (i, 0)),
        out_shape=jax.ShapeDtypeStruct((m, d), out_dtype),
        compiler_params=_params("arbitrary"),
        name="rmsnorm",
    )(x, g.reshape(1, d))


def _matmul_kernel(*refs, k_sizes, has_res):
    na = len(k_sizes)
    a_refs = refs[:na]
    w_ref = refs[na]
    r_ref = refs[na + 1] if has_res else None
    o_ref, wb_ref = refs[-2], refs[-1]

    @pl.when(pl.program_id(1) == 0)
    def _():
        wb_ref[...] = w_ref[...].astype(BF16)

    acc = None
    k0 = 0
    for a_ref, ks in zip(a_refs, k_sizes):
        part = jnp.dot(a_ref[...], wb_ref[k0:k0 + ks, :], preferred_element_type=F32)
        acc = part if acc is None else acc + part
        k0 += ks
    if has_res:
        acc = acc + r_ref[...]
    o_ref[...] = acc.astype(o_ref.dtype)


def _matmul(a_list, w, w_lead=(), *, n_cols=None, residual=None, out_dtype=F32,
            tm=1024, tn=512, name="matmul"):
    m = a_list[0].shape[0]
    k_sizes = tuple(a.shape[1] for a in a_list)
    k = sum(k_sizes)
    n = w.shape[-1] if n_cols is None else n_cols
    tm = min(tm, m)
    tn = min(tn, n)
    nlead = len(w_lead)
    w_block = (None,) * nlead + (k, tn)
    w_mode = dict(pipeline_mode=pl.Buffered(1)) if n == tn else {}
    in_specs = [pl.BlockSpec((tm, ks), lambda j, i: (i, 0)) for ks in k_sizes]
    in_specs.append(pl.BlockSpec(w_block, lambda j, i: tuple(w_lead) + (0, j), **w_mode))
    args = list(a_list) + [w]
    if residual is not None:
        in_specs.append(pl.BlockSpec((tm, tn), lambda j, i: (i, j)))
        args.append(residual)
    return pl.pallas_call(
        functools.partial(_matmul_kernel, k_sizes=k_sizes, has_res=residual is not None),
        grid=(n // tn, m // tm),
        in_specs=in_specs,
        out_specs=pl.BlockSpec((tm, tn), lambda j, i: (i, j)),
        out_shape=jax.ShapeDtypeStruct((m, n), out_dtype),
        scratch_shapes=[pltpu.VMEM((k, tn), BF16)],
        compiler_params=_params("arbitrary", "arbitrary"),
        name=name,
    )(*args)


def _gate_cols_kernel(w_ref, o_ref, *, col0, ncols):
    o_ref[...] = jnp.zeros(o_ref.shape, o_ref.dtype)
    o_ref[:, 0:ncols] = w_ref[:, col0:col0 + ncols]


def _gate_cols(w, lead, col0, ncols, tk=256):
    k, n = w.shape[-2:]
    return pl.pallas_call(
        functools.partial(_gate_cols_kernel, col0=col0, ncols=ncols),
        grid=(k // tk,),
        in_specs=[pl.BlockSpec((None, tk, n), lambda i: (lead, i, 0))],
        out_specs=pl.BlockSpec((tk, LANES), lambda i: (i, 0)),
        out_shape=jax.ShapeDtypeStruct((k, LANES), F32),
        compiler_params=_params("arbitrary"),
        name="gate_cols",
    )(w)


CONV_HALO = 32
CONV_BLOCK = 64
CONV_FIRST = CONV_HALO - (CONV_WIDTH - 1)


def _conformer_kernel(a_ref, g_ref, w_ref, b_ref, lng_ref, lnb_ref, o_ref, ubuf, sbuf, cbuf, *, ts):
    s = pl.program_id(1)
    c = a_ref.shape[1]

    @pl.when(s == 0)
    def _():
        ubuf[0:CONV_HALO, :] = jnp.zeros((CONV_HALO, c), F32)

    @pl.when(s > 0)
    def _():
        ubuf[0:CONV_HALO, :] = ubuf[ts:ts + CONV_HALO, :]

    ubuf[CONV_HALO:CONV_HALO + ts, :] = a_ref[...] * _sigmoid(g_ref[...])
    for sft in range(1, SUBLANES):
        sbuf[sft - 1] = ubuf[sft:sft + sbuf.shape[1], :]
    lng = lng_ref[...]
    lnb = lnb_ref[...]
    nsub = CONV_BLOCK // SUBLANES
    for r in range(ts // CONV_BLOCK):
        base = r * CONV_BLOCK
        for lc in range(c // LANES):
            lanes = slice(lc * LANES, (lc + 1) * LANES)
            accs = [jnp.broadcast_to(b_ref[:, lanes], (SUBLANES, LANES))] * nsub
            for k in range(CONV_WIDTH):
                sft = (CONV_FIRST + k) % SUBLANES
                row = base + (CONV_FIRST + k) - sft
                wk = jnp.broadcast_to(w_ref[k:k + 1, lanes], (SUBLANES, LANES))
                for i in range(nsub):
                    lo = row + i * SUBLANES
                    if sft == 0:
                        tap = ubuf[lo:lo + SUBLANES, lanes]
                    else:
                        tap = sbuf[sft - 1, lo:lo + SUBLANES, lanes]
                    accs[i] = accs[i] + wk * tap
            for i in range(nsub):
                cbuf[base + i * SUBLANES:base + (i + 1) * SUBLANES, lanes] = accs[i]
        y = cbuf[base:base + CONV_BLOCK, :]
        mu = jnp.mean(y, axis=-1, keepdims=True)
        d = y - mu
        var = jnp.mean(d * d, axis=-1, keepdims=True)
        y = d * lax.rsqrt(var + EPS) * lng + lnb
        o_ref[base:base + CONV_BLOCK, :] = (y * _sigmoid(y)).astype(o_ref.dtype)


def _conformer(z, w_dw, b_dw, ln_g, ln_b, batch, seq, ts=256):
    m = z.shape[0]
    c = w_dw.shape[1]
    ts = min(ts, seq)
    nt = seq // ts
    row = lambda b, s: (b * nt + s, 0)
    vec = pl.BlockSpec((1, c), lambda b, s: (0, 0))
    return pl.pallas_call(
        functools.partial(_conformer_kernel, ts=ts),
        grid=(batch, nt),
        in_specs=[pl.BlockSpec((ts, c), row),
                  pl.BlockSpec((ts, c), lambda b, s: (b * nt + s, 1)),
                  pl.BlockSpec((CONV_WIDTH, c), lambda b, s: (0, 0)),
                  vec, vec, vec],
        out_specs=pl.BlockSpec((ts, c), row),
        out_shape=jax.ShapeDtypeStruct((m, c), BF16),
        scratch_shapes=[pltpu.VMEM((CONV_HALO + ts, c), F32),
                        pltpu.VMEM((SUBLANES - 1, ts + CONV_HALO - SUBLANES, c), F32),
                        pltpu.VMEM((ts, c), F32)],
        compiler_params=_params("arbitrary", "arbitrary"),
        name="conformer_conv",
    )(z, z, w_dw, b_dw.reshape(1, c), ln_g.reshape(1, c), ln_b.reshape(1, c))


QK_HALO = 8


def _mlstm_kernel(q_ref, k_ref, v_ref, o_ref, zg_ref, zgt_ref, cwq_ref, cwk_ref, cbq_ref, cbk_ref,
                  bcol_ref, brow_ref, ng_ref, out_ref, qbuf, kbuf, c_scr, n_scr, m_scr, *, nh, dh):
    ci = pl.program_id(1)
    L = q_ref.shape[0]
    w = q_ref.shape[1]

    @pl.when(ci == 0)
    def _():
        qbuf[0:QK_HALO, :] = jnp.zeros((QK_HALO, w), F32)
        kbuf[0:QK_HALO, :] = jnp.zeros((QK_HALO, w), F32)
        c_scr[...] = jnp.zeros(c_scr.shape, F32)
        n_scr[...] = jnp.zeros(n_scr.shape, F32)
        m_scr[...] = jnp.zeros(m_scr.shape, F32)

    @pl.when(ci > 0)
    def _():
        qbuf[0:QK_HALO, :] = qbuf[L:L + QK_HALO, :]
        kbuf[0:QK_HALO, :] = kbuf[L:L + QK_HALO, :]

    qbuf[QK_HALO:QK_HALO + L, :] = q_ref[...]
    kbuf[QK_HALO:QK_HALO + L, :] = k_ref[...]

    def short_conv(buf, cw_ref, cb_ref, lo, hi):
        acc = jnp.broadcast_to(cb_ref[:, lo:hi], (L, hi - lo))
        for j in range(QK_CONV_WIDTH):
            off = QK_HALO - (QK_CONV_WIDTH - 1) + j
            acc = acc + cw_ref[j:j + 1, lo:hi] * buf[off:off + L, lo:hi]
        return acc * _sigmoid(acc)

    zg = zg_ref[...] + bcol_ref[...]
    zgt = zgt_ref[...] + brow_ref[...]
    lf_c = _log_sigmoid(zg)
    lf_r = _log_sigmoid(zgt)
    row_i = lax.broadcasted_iota(jnp.int32, (L, L), 0)
    col_i = lax.broadcasted_iota(jnp.int32, (L, L), 1)
    causal = row_i >= col_i
    scale = dh ** -0.5
    nt_dims = (((1,), (1,)), ((), ()))
    tn_dims = (((0,), (0,)), ((), ()))

    for h in range(nh):
        lo, hi = h * dh, (h + 1) * dh
        li_col = zg[:, h:h + 1]
        lf_col = lf_c[:, nh + h:nh + h + 1]
        li_row = zgt[h:h + 1, :]
        lf_row = lf_r[nh + h:nh + h + 1, :]
        b_col = jnp.sum(jnp.where(causal, lf_row, 0.0), axis=1, keepdims=True)
        b_row = jnp.sum(jnp.where(row_i <= col_i, lf_col, 0.0), axis=0, keepdims=True)
        g = jnp.sum(lf_row, axis=1, keepdims=True)
        m_prev = m_scr[h:h + 1, 0:1]

        qh = short_conv(qbuf, cwq_ref, cbq_ref, lo, hi)
        kh = short_conv(kbuf, cwk_ref, cbk_ref, lo, hi) * scale
        vh = v_ref[:, lo:hi]
        qb = qh.astype(BF16)
        kb = kh.astype(BF16)
        vb = vh.astype(BF16)

        a_col = b_col + m_prev
        logw = jnp.where(causal, b_col - b_row + li_row, -jnp.inf)
        m_q = jnp.maximum(a_col, jnp.max(logw, axis=1, keepdims=True))
        wgt = jnp.exp(logw - m_q)
        inter = jnp.exp(a_col - m_q)
        s = lax.dot_general(qb, kb, nt_dims, preferred_element_type=F32) * wgt
        c_prev = c_scr[h]
        n_prev = n_scr[h:h + 1, :]
        num = (inter * jnp.dot(qb, c_prev.astype(BF16), preferred_element_type=F32)
               + jnp.dot(s.astype(BF16), vb, preferred_element_type=F32))
        den = (inter * jnp.sum(qh * n_prev, axis=1, keepdims=True)
               + jnp.sum(s, axis=1, keepdims=True))
        hout = num / jnp.maximum(jnp.abs(den), jnp.exp(-m_q))

        logu = g - b_col + li_col
        m_new = jnp.maximum(g + m_prev, jnp.max(logu, axis=0, keepdims=True))
        decay = jnp.exp(g + m_prev - m_new)
        ku = kh * jnp.exp(logu - m_new)
        c_scr[h] = decay * c_prev + lax.dot_general(ku.astype(BF16), vb, tn_dims,
                                                    preferred_element_type=F32)
        n_scr[h:h + 1, :] = decay * n_prev + jnp.sum(ku, axis=0, keepdims=True)
        m_scr[h:h + 1, :] = jnp.broadcast_to(m_new, (1, m_scr.shape[1]))

        ht = _sigmoid(o_ref[:, lo:hi]) * hout
        mu = jnp.mean(ht, axis=-1, keepdims=True)
        d = ht - mu
        var = jnp.mean(d * d, axis=-1, keepdims=True)
        out_ref[:, lo:hi] = (d * lax.rsqrt(var + EPS) * ng_ref[:, lo:hi]).astype(out_ref.dtype)


def _mlstm(z, zg, zgt, qk_w, qk_b, b_i, b_f, norm_g, batch, seq):
    m = z.shape[0]
    nh = N_MLSTM_HEADS
    w = norm_g.shape[0]
    dh = w // nh
    L = CHUNK
    nc = seq // L
    blk = lambda col: pl.BlockSpec((L, w), lambda b, c: (b * nc + c, col))
    full = lambda shape: pl.BlockSpec(shape, lambda b, c: (0,) * len(shape))
    bias = jnp.concatenate([b_i, b_f]).astype(F32)
    bcol = jnp.zeros((1, LANES), F32).at[0, :2 * nh].set(bias)
    brow = jnp.broadcast_to(bias[:, None], (2 * nh, L))
    return pl.pallas_call(
        functools.partial(_mlstm_kernel, nh=nh, dh=dh),
        grid=(batch, nc),
        in_specs=[blk(2), blk(3), blk(4), blk(5),
                  pl.BlockSpec((L, LANES), lambda b, c: (b * nc + c, 0)),
                  pl.BlockSpec((2 * nh, L), lambda b, c: (0, b * nc + c)),
                  full((QK_CONV_WIDTH, w)), full((QK_CONV_WIDTH, w)),
                  full((1, w)), full((1, w)),
                  full((1, LANES)), full((2 * nh, L)), full((1, w))],
        out_specs=pl.BlockSpec((L, w), lambda b, c: (b * nc + c, 0)),
        out_shape=jax.ShapeDtypeStruct((m, w), BF16),
        scratch_shapes=[pltpu.VMEM((QK_HALO + L, w), F32),
                        pltpu.VMEM((QK_HALO + L, w), F32),
                        pltpu.VMEM((nh, dh, dh), F32),
                        pltpu.VMEM((SUBLANES, dh), F32),
                        pltpu.VMEM((SUBLANES, LANES), F32)],
        compiler_params=_params("arbitrary", "arbitrary"),
        name="mlstm",
    )(z, z, z, z, zg, zgt, qk_w[:, :w], qk_w[:, w:], qk_b[:w].reshape(1, w), qk_b[w:].reshape(1, w),
      bcol, brow, norm_g.reshape(1, w))


def _xattn_kernel(q_ref, k_ref, v_ref, o_ref, *, nh):
    d = q_ref.shape[1]
    dh = d // nh
    scale = dh ** -0.5
    nt_dims = (((1,), (1,)), ((), ()))
    for h in range(nh):
        lo, hi = h * dh, (h + 1) * dh
        sc = lax.dot_general(q_ref[:, lo:hi], k_ref[:, lo:hi], nt_dims,
                             preferred_element_type=F32) * scale
        mx = jnp.max(sc, axis=-1, keepdims=True)
        e = jnp.exp(sc - mx)
        p = e / jnp.sum(e, axis=-1, keepdims=True)
        o_ref[:, lo:hi] = jnp.dot(p.astype(BF16), v_ref[:, lo:hi],
                                  preferred_element_type=F32).astype(o_ref.dtype)


def _xattn(q, k, v, batch, seq, mem_len, ts=512):
    m, d = q.shape
    ts = min(ts, seq)
    nt = seq // ts
    return pl.pallas_call(
        functools.partial(_xattn_kernel, nh=N_XHEADS),
        grid=(batch, nt),
        in_specs=[pl.BlockSpec((ts, d), lambda b, s: (b * nt + s, 0)),
                  pl.BlockSpec((mem_len, d), lambda b, s: (b, 0)),
                  pl.BlockSpec((mem_len, d), lambda b, s: (b, 0))],
        out_specs=pl.BlockSpec((ts, d), lambda b, s: (b * nt + s, 0)),
        out_shape=jax.ShapeDtypeStruct((m, d), BF16),
        compiler_params=_params("arbitrary", "arbitrary"),
        name="cross_attn",
    )(q, k, v)


def _swiglu_tile(h_ref, wgb_ref, wub_ref, o_ref):
    h = h_ref[...]
    g = jnp.dot(h, wgb_ref[...], preferred_element_type=F32)
    u = jnp.dot(h, wub_ref[...], preferred_element_type=F32)
    o_ref[...] = (g * _sigmoid(g) * u).astype(o_ref.dtype)


def _gateup_kernel(h_ref, wg_ref, wu_ref, o_ref, wgb_ref, wub_ref):
    @pl.when(pl.program_id(1) == 0)
    def _():
        wgb_ref[...] = wg_ref[...].astype(BF16)
        wub_ref[...] = wu_ref[...].astype(BF16)

    _swiglu_tile(h_ref, wgb_ref, wub_ref, o_ref)


def _gateup(h, wg, wu, w_lead, tm=1024, tc=512):
    m, d = h.shape
    f = wg.shape[-1]
    tm = min(tm, m)
    nlead = len(w_lead)
    wspec = pl.BlockSpec((None,) * nlead + (d, tc), lambda c, i: tuple(w_lead) + (0, c))
    return pl.pallas_call(
        _gateup_kernel,
        grid=(f // tc, m // tm),
        in_specs=[pl.BlockSpec((tm, d), lambda c, i: (i, 0)), wspec, wspec],
        out_specs=pl.BlockSpec((tm, tc), lambda c, i: (i, c)),
        out_shape=jax.ShapeDtypeStruct((m, f), BF16),
        scratch_shapes=[pltpu.VMEM((d, tc), BF16), pltpu.VMEM((d, tc), BF16)],
        compiler_params=_params("arbitrary", "arbitrary"),
        name="swiglu_gateup",
    )(h, wg, wu)


def _expert_changed(te_ref, i):
    return jnp.logical_or(i == 0, te_ref[i] != te_ref[jnp.maximum(i - 1, 0)])


def _moe_gateup_kernel(te_ref, ta_ref, h_ref, wg_ref, wu_ref, o_ref, wgb_ref, wub_ref):
    i = pl.program_id(1)

    @pl.when(_expert_changed(te_ref, i))
    def _():
        wgb_ref[...] = wg_ref[...].astype(BF16)
        wub_ref[...] = wu_ref[...].astype(BF16)

    @pl.when(ta_ref[i] > 0)
    def _():
        _swiglu_tile(h_ref, wgb_ref, wub_ref, o_ref)

    @pl.when(ta_ref[i] == 0)
    def _():
        o_ref[...] = jnp.zeros(o_ref.shape, o_ref.dtype)


def _moe_gateup(tile_e, tile_on, hs, wg, wu, layer, tc=512):
    r, d = hs.shape
    f = wg.shape[-1]
    tg = MOE_TILE
    wspec = pl.BlockSpec((None, None, d, tc), lambda c, i, te, ta: (layer, te[i], 0, c))
    return pl.pallas_call(
        _moe_gateup_kernel,
        grid_spec=pltpu.PrefetchScalarGridSpec(
            num_scalar_prefetch=2,
            grid=(f // tc, r // tg),
            in_specs=[pl.BlockSpec((tg, d), lambda c, i, te, ta: (i, 0)), wspec, wspec],
            out_specs=pl.BlockSpec((tg, tc), lambda c, i, te, ta: (i, c)),
            scratch_shapes=[pltpu.VMEM((d, tc), BF16), pltpu.VMEM((d, tc), BF16)]),
        out_shape=jax.ShapeDtypeStruct((r, f), BF16),
        compiler_params=_params("arbitrary", "arbitrary"),
        name="moe_gateup",
    )(tile_e, tile_on, hs, wg, wu)


def _moe_down_kernel(te_ref, ta_ref, a_ref, w_ref, o_ref, wb_ref):
    i = pl.program_id(1)

    @pl.when(_expert_changed(te_ref, i))
    def _():
        wb_ref[...] = w_ref[...].astype(BF16)

    @pl.when(ta_ref[i] > 0)
    def _():
        o_ref[...] = jnp.dot(a_ref[...], wb_ref[...], preferred_element_type=F32)

    @pl.when(ta_ref[i] == 0)
    def _():
        o_ref[...] = jnp.zeros(o_ref.shape, o_ref.dtype)


def _moe_down(tile_e, tile_on, a, wd, layer, tn=512):
    r, f = a.shape
    d = wd.shape[-1]
    tg = MOE_TILE
    return pl.pallas_call(
        _moe_down_kernel,
        grid_spec=pltpu.PrefetchScalarGridSpec(
            num_scalar_prefetch=2,
            grid=(d // tn, r // tg),
            in_specs=[pl.BlockSpec((tg, f), lambda j, i, te, ta: (i, 0)),
                      pl.BlockSpec((None, None, f, tn), lambda j, i, te, ta: (layer, te[i], 0, j))],
            out_specs=pl.BlockSpec((tg, tn), lambda j, i, te, ta: (i, j)),
            scratch_shapes=[pltpu.VMEM((f, tn), BF16)]),
        out_shape=jax.ShapeDtypeStruct((r, d), F32),
        compiler_params=_params("arbitrary", "arbitrary"),
        name="moe_down",
    )(tile_e, tile_on, a, wd)


def _router_kernel(h_ref, wr_ref, br_ref, o_ref, *, ne):
    logits = jnp.dot(h_ref[...], wr_ref[...].astype(BF16), preferred_element_type=F32) + br_ref[...]
    lane = lax.broadcasted_iota(jnp.int32, logits.shape, 1).astype(F32)
    lg = jnp.where(lane < ne, logits, -jnp.inf)
    v1 = jnp.max(lg, axis=1, keepdims=True)
    i1 = jnp.min(jnp.where(lg == v1, lane, float(LANES)), axis=1, keepdims=True)
    lg2 = jnp.where(lane == i1, -jnp.inf, lg)
    v2 = jnp.max(lg2, axis=1, keepdims=True)
    i2 = jnp.min(jnp.where(lg2 == v2, lane, float(LANES)), axis=1, keepdims=True)
    e2 = jnp.exp(v2 - v1)
    w1 = 1.0 / (1.0 + e2)
    w2 = e2 / (1.0 + e2)
    o_ref[...] = jnp.where(lane == 0, i1, jnp.where(lane == 1, i2,
                           jnp.where(lane == 2, w1, jnp.where(lane == 3, w2, 0.0))))


def _router(h, w_r, b_r, tm=1024):
    m, d = h.shape
    ne = w_r.shape[1]
    tm = min(tm, m)
    wr = jnp.zeros((d, LANES), F32).at[:, :ne].set(w_r)
    br = jnp.zeros((1, LANES), F32).at[0, :ne].set(b_r)
    return pl.pallas_call(
        functools.partial(_router_kernel, ne=ne),
        grid=(m // tm,),
        in_specs=[pl.BlockSpec((tm, d), lambda i: (i, 0)),
                  pl.BlockSpec((d, LANES), lambda i: (0, 0)),
                  pl.BlockSpec((1, LANES), lambda i: (0, 0))],
        out_specs=pl.BlockSpec((tm, LANES), lambda i: (i, 0)),
        out_shape=jax.ShapeDtypeStruct((m, LANES), F32),
        compiler_params=_params("arbitrary"),
        name="moe_router",
    )(h, wr, br)


def _row_copy(src_hbm, row, buf, slot, sem):
    return pltpu.make_async_copy(src_hbm.at[pl.ds(row, 1), :], buf.at[pl.ds(slot, 1), :], sem)


def _dispatch_kernel(src_ref, nrows_ref, x_hbm, g_ref, o_ref, buf, sem, *, rows):
    base = pl.program_id(0) * rows
    active = base < nrows_ref[0]

    @pl.when(active)
    def _():
        def start(c, carry):
            for u in range(DMA_UNROLL):
                r = c * DMA_UNROLL + u
                _row_copy(x_hbm, src_ref[base + r], buf, r, sem).start(priority=u % 2)
            return carry

        def wait(c, carry):
            for u in range(DMA_UNROLL):
                _row_copy(x_hbm, 0, buf, c * DMA_UNROLL + u, sem).wait()
            return carry

        lax.fori_loop(0, rows // DMA_UNROLL, start, 0)
        lax.fori_loop(0, rows // DMA_UNROLL, wait, 0)
        x = buf[...]
        ms = jnp.mean(x * x, axis=-1, keepdims=True)
        o_ref[...] = (x * lax.rsqrt(ms + EPS) * g_ref[...]).astype(o_ref.dtype)

    @pl.when(jnp.logical_not(active))
    def _():
        o_ref[...] = jnp.zeros(o_ref.shape, o_ref.dtype)


def _dispatch(src, n_used, x, g, n_rows):
    m, d = x.shape
    rows = GATHER_ROWS
    return pl.pallas_call(
        functools.partial(_dispatch_kernel, rows=rows),
        grid_spec=pltpu.PrefetchScalarGridSpec(
            num_scalar_prefetch=2,
            grid=(n_rows // rows,),
            in_specs=[pl.BlockSpec(memory_space=pl.ANY),
                      pl.BlockSpec((1, d), lambda i, s, n: (0, 0))],
            out_specs=pl.BlockSpec((rows, d), lambda i, s, n: (i, 0)),
            scratch_shapes=[pltpu.VMEM((rows, d), F32), pltpu.SemaphoreType.DMA(())]),
        out_shape=jax.ShapeDtypeStruct((n_rows, d), BF16),
        compiler_params=_params("arbitrary"),
        name="moe_dispatch",
    )(src, n_used, x, g.reshape(1, d))


def _combine_kernel(p1_ref, p2_ref, x_ref, route_ref, y_hbm, o_ref, buf1, buf2, sem, *, rows):
    base = pl.program_id(0) * rows

    def start(c, carry):
        for u in range(DMA_UNROLL):
            r = c * DMA_UNROLL + u
            _row_copy(y_hbm, p1_ref[base + r], buf1, r, sem.at[0]).start(priority=0)
            _row_copy(y_hbm, p2_ref[base + r], buf2, r, sem.at[1]).start(priority=1)
        return carry

    def wait(c, carry):
        for u in range(DMA_UNROLL):
            r = c * DMA_UNROLL + u
            _row_copy(y_hbm, 0, buf1, r, sem.at[0]).wait()
            _row_copy(y_hbm, 0, buf2, r, sem.at[1]).wait()
        return carry

    lax.fori_loop(0, rows // DMA_UNROLL, start, 0)
    lax.fori_loop(0, rows // DMA_UNROLL, wait, 0)
    route = route_ref[...]
    o_ref[...] = x_ref[...] + route[:, 2:3] * buf1[...] + route[:, 3:4] * buf2[...]


def _combine(p1, p2, x, route, y):
    m, d = x.shape
    rows = GATHER_ROWS
    return pl.pallas_call(
        functools.partial(_combine_kernel, rows=rows),
        grid_spec=pltpu.PrefetchScalarGridSpec(
            num_scalar_prefetch=2,
            grid=(m // rows,),
            in_specs=[pl.BlockSpec((rows, d), lambda i, a, b: (i, 0)),
                      pl.BlockSpec((rows, LANES), lambda i, a, b: (i, 0)),
                      pl.BlockSpec(memory_space=pl.ANY)],
            out_specs=pl.BlockSpec((rows, d), lambda i, a, b: (i, 0)),
            scratch_shapes=[pltpu.VMEM((rows, d), F32), pltpu.VMEM((rows, d), F32),
                            pltpu.SemaphoreType.DMA((2,))]),
        out_shape=jax.ShapeDtypeStruct((m, d), F32),
        compiler_params=_params("arbitrary"),
        name="moe_combine",
    )(p1, p2, x, route, y)


def _moe_plan(route, m, ne, tg):
    i1 = route[:, 0].astype(jnp.int32)
    i2 = route[:, 1].astype(jnp.int32)
    experts = jnp.arange(ne, dtype=jnp.int32)
    sel = ((i1[:, None] == experts) | (i2[:, None] == experts)).astype(jnp.int32)
    counts = jnp.sum(sel, axis=0)
    padded = ((counts + tg - 1) // tg) * tg
    ends = jnp.cumsum(padded)
    pos = (ends - padded)[None, :] + jnp.cumsum(sel, axis=0) - sel
    p1 = jnp.take_along_axis(pos, i1[:, None], axis=1)[:, 0]
    p2 = jnp.take_along_axis(pos, i2[:, None], axis=1)[:, 0]
    n_tiles = (2 * m) // tg + ne
    tok = jnp.arange(m, dtype=jnp.int32)
    src = jnp.zeros((n_tiles * tg,), jnp.int32).at[p1].set(tok).at[p2].set(tok)
    tile_start = jnp.arange(n_tiles, dtype=jnp.int32) * tg
    tile_e = jnp.minimum(jnp.sum((tile_start[:, None] >= ends[None, :]).astype(jnp.int32), axis=1), ne - 1)
    tile_on = (tile_start < ends[-1]).astype(jnp.int32)
    return p1, p2, src, ends[-1:], tile_e, tile_on, n_tiles * tg


def _moe_block(x, g_ffn, w_r, b_r, wg, wu, wd, layer):
    m, d = x.shape
    h = _rmsnorm(x, g_ffn, BF16)
    route = _router(h, w_r, b_r)
    p1, p2, src, n_used, tile_e, tile_on, n_rows = _moe_plan(route, m, N_EXPERTS, MOE_TILE)
    hs = _dispatch(src, n_used, x, g_ffn, n_rows)
    a = _moe_gateup(tile_e, tile_on, hs, wg, wu, layer)
    y = _moe_down(tile_e, tile_on, a, wd, layer)
    return _combine(p1, p2, x, route, y)


def kernel(x, mem, norm_mix, w_in, conv_dw_w, conv_dw_b, conv_ln_g, conv_ln_b, qk_conv_w, qk_conv_b, b_igate, b_fgate, mlstm_norm_g, w_out, norm_cross, norm_mem, w_cq, w_ck, w_cv, w_co, norm_ffn, w_gate_dense, w_up_dense, w_down_dense, w_router, b_router, w_gate_moe, w_up_moe, w_down_moe, norm_final):
    batch, seq, d = x.shape
    mem_len = mem.shape[1]
    depth = norm_mix.shape[0]
    d_conv = conv_dw_w.shape[2]
    d_mlstm = mlstm_norm_g.shape[1]
    n_main = 2 * d_conv + 4 * d_mlstm
    nh = N_MLSTM_HEADS
    xf = x.reshape(batch * seq, d)
    memf = mem.reshape(batch * mem_len, d)

    for l in range(depth):
        h = _rmsnorm(xf, norm_mix[l], BF16)
        z = _matmul([h], w_in, (l,), n_cols=n_main, tn=1024, name="w_in")
        zg = _matmul([h], _gate_cols(w_in, l, n_main, 2 * nh), tn=LANES, name="w_in_gates")
        zgt = zg[:, :2 * nh].T
        yc = _conformer(z, conv_dw_w[l], conv_dw_b[l], conv_ln_g[l], conv_ln_b[l], batch, seq)
        ym = _mlstm(z, zg, zgt, qk_conv_w[l], qk_conv_b[l], b_igate[l], b_fgate[l], mlstm_norm_g[l],
                    batch, seq)
        xf = _matmul([yc, ym], w_out, (l,), residual=xf, tm=512, tn=d, name="w_out")
        h = _rmsnorm(xf, norm_cross[l], BF16)
        mem_n = _rmsnorm(memf, norm_mem[l], BF16)
        q = _matmul([h], w_cq, (l,), out_dtype=BF16, tn=1024, name="w_cq")
        kk = _matmul([mem_n], w_ck, (l,), out_dtype=BF16, name="w_ck")
        vv = _matmul([mem_n], w_cv, (l,), out_dtype=BF16, name="w_cv")
        att = _xattn(q, kk, vv, batch, seq, mem_len)
        xf = _matmul([att], w_co, (l,), residual=xf, tm=512, tn=d, name="w_co")
        j = l // 2
        if l % 2 == 0:
            h = _rmsnorm(xf, norm_ffn[l], BF16)
            a = _gateup(h, w_gate_dense, w_up_dense, (j,))
            xf = _matmul([a], w_down_dense, (j,), residual=xf, tm=512, tn=512, name="w_down")
        else:
            xf = _moe_block(xf, norm_ffn[l], w_router[j], b_router[j],
                            w_gate_moe, w_up_moe, w_down_moe, j)
    return _rmsnorm(xf, norm_final, F32).reshape(batch, seq, d)
```

```python
import functools

import jax
import jax.numpy as jnp
from jax import lax
from jax.experimental import pallas as pl
from jax.experimental.pallas import tpu as pltpu

F32 = jnp.float32
BF16 = jnp.bfloat16
EPS = 1e-6

V7X_VMEM_BYTES = 64 * 1024 * 1024
VMEM_LIMIT = V7X_VMEM_BYTES - 8 * 1024 * 1024
LANES = 128
SUBLANES = 8

N_MLSTM_HEADS = 4
N_XHEADS = 4
CONV_WIDTH = 31
QK_CONV_WIDTH = 4
CHUNK = 128
N_EXPERTS = 8
MOE_TILE = 512
GATHER_ROWS = 512
DMA_UNROLL = 8


def _params(*sem):
    return pltpu.CompilerParams(dimension_semantics=sem, vmem_limit_bytes=VMEM_LIMIT)


def _sigmoid(x):
    return 1.0 / (1.0 + jnp.exp(-x))


def _log_sigmoid(x):
    return jnp.minimum(x, 0.0) - jnp.log(1.0 + jnp.exp(-jnp.abs(x)))


def _rmsnorm_kernel(x_ref, g_ref, o_ref):
    x = x_ref[...]
    ms = jnp.mean(x * x, axis=-1, keepdims=True)
    o_ref[...] = (x * lax.rsqrt(ms + EPS) * g_ref[...]).astype(o_ref.dtype)


def _rmsnorm(x, g, out_dtype, tm=512):
    m, d = x.shape
    tm = min(tm, m)
    return pl.pallas_call(
        _rmsnorm_kernel,
        grid=(m // tm,),
        in_specs=[pl.BlockSpec((tm, d), lambda i: (i, 0)),
                  pl.BlockSpec((1, d), lambda i: (0, 0))],
        out_specs=pl.BlockSpec((tm, d), lambda i: (i, 0)),
        out_shape=jax.ShapeDtypeStruct((m, d), out_dtype),
        compiler_params=_params("arbitrary"),
        name="rmsnorm",
    )(x, g.reshape(1, d))


NT_DIMS = (((1,), (1,)), ((), ()))


def _matmul_kernel(*refs, k_sizes, w_is_nk, has_res, has_norm):
    na = len(k_sizes)
    a_refs = refs[:na]
    w_ref = refs[na]
    pos = na + 1
    r_ref = g_ref = hn_ref = None
    if has_res:
        r_ref = refs[pos]
        pos += 1
    if has_norm:
        g_ref = refs[pos]
        pos += 1
    o_ref = refs[pos]
    if has_norm:
        hn_ref = refs[pos + 1]
    wb_ref = refs[-1]

    @pl.when(pl.program_id(1) == 0)
    def _():
        wb_ref[...] = w_ref[...].astype(BF16)

    acc = None
    k0 = 0
    for a_ref, ks in zip(a_refs, k_sizes):
        if w_is_nk:
            part = lax.dot_general(a_ref[...], wb_ref[:, k0:k0 + ks], NT_DIMS, preferred_element_type=F32)
        else:
            part = jnp.dot(a_ref[...], wb_ref[k0:k0 + ks, :], preferred_element_type=F32)
        acc = part if acc is None else acc + part
        k0 += ks
    if has_res:
        acc = acc + r_ref[...]
    o_ref[...] = acc.astype(o_ref.dtype)
    if has_norm:
        ms = jnp.mean(acc * acc, axis=-1, keepdims=True)
        hn_ref[...] = (acc * lax.rsqrt(ms + EPS) * g_ref[...]).astype(hn_ref.dtype)


def _matmul(a_list, w, w_lead=(), *, n_cols=None, w_is_nk=False, residual=None, norm_gain=None,
            out_dtype=F32, tm=1024, tn=512, name="matmul"):
    m = a_list[0].shape[0]
    k_sizes = tuple(a.shape[1] for a in a_list)
    k = sum(k_sizes)
    n_total = w.shape[-2] if w_is_nk else w.shape[-1]
    n = n_total if n_cols is None else n_cols
    tm = min(tm, m)
    tn = min(tn, n)
    nlead = len(w_lead)
    lead = tuple(w_lead)
    if w_is_nk:
        w_block, w_map, wb_shape = (None,) * nlead + (tn, k), (lambda j, i: lead + (j, 0)), (tn, k)
    else:
        w_block, w_map, wb_shape = (None,) * nlead + (k, tn), (lambda j, i: lead + (0, j)), (k, tn)
    w_mode = dict(pipeline_mode=pl.Buffered(1)) if n == tn else {}
    in_specs = [pl.BlockSpec((tm, ks), lambda j, i: (i, 0)) for ks in k_sizes]
    in_specs.append(pl.BlockSpec(w_block, w_map, **w_mode))
    args = list(a_list) + [w]
    out_tile = pl.BlockSpec((tm, tn), lambda j, i: (i, j))
    if residual is not None:
        in_specs.append(out_tile)
        args.append(residual)
    out_specs, out_shape = out_tile, jax.ShapeDtypeStruct((m, n), out_dtype)
    if norm_gain is not None:
        assert tn == n, "the fused RMSNorm needs whole rows"
        in_specs.append(pl.BlockSpec((1, n), lambda j, i: (0, 0)))
        args.append(norm_gain.reshape(1, n))
        out_specs, out_shape = [out_tile, out_tile], [out_shape, jax.ShapeDtypeStruct((m, n), BF16)]
    return pl.pallas_call(
        functools.partial(_matmul_kernel, k_sizes=k_sizes, w_is_nk=w_is_nk,
                          has_res=residual is not None, has_norm=norm_gain is not None),
        grid=(n // tn, m // tm),
        in_specs=in_specs,
        out_specs=out_specs,
        out_shape=out_shape,
        scratch_shapes=[pltpu.VMEM(wb_shape, BF16)],
        compiler_params=_params("arbitrary", "arbitrary"),
        name=name,
    )(*args)


def _gates_kernel(h_ref, w_ref, o_ref):
    o_ref[...] = lax.dot_general(w_ref[...].astype(BF16), h_ref[...], NT_DIMS, preferred_element_type=F32)


def _gates(h, w_nk, lead, row0, nrows, tm=1024):
    m, k = h.shape
    tm = min(tm, m)
    assert row0 % nrows == 0
    return pl.pallas_call(
        _gates_kernel,
        grid=(m // tm,),
        in_specs=[pl.BlockSpec((tm, k), lambda i: (i, 0)),
                  pl.BlockSpec((None, nrows, k), lambda i: (lead, row0 // nrows, 0))],
        out_specs=pl.BlockSpec((nrows, tm), lambda i: (0, i)),
        out_shape=jax.ShapeDtypeStruct((nrows, m), F32),
        compiler_params=_params("arbitrary"),
        name="w_in_gates",
    )(h, w_nk)


CONV_HALO = 32
CONV_BLOCK = 64
CONV_FIRST = CONV_HALO - (CONV_WIDTH - 1)


def _conformer_kernel(a_ref, g_ref, w_ref, b_ref, lng_ref, lnb_ref, o_ref, ubuf, sbuf, cbuf, *, ts):
    s = pl.program_id(1)
    c = a_ref.shape[1]

    @pl.when(s == 0)
    def _():
        ubuf[0:CONV_HALO, :] = jnp.zeros((CONV_HALO, c), F32)

    @pl.when(s > 0)
    def _():
        ubuf[0:CONV_HALO, :] = ubuf[ts:ts + CONV_HALO, :]

    ubuf[CONV_HALO:CONV_HALO + ts, :] = a_ref[...] * _sigmoid(g_ref[...])
    for sft in range(1, SUBLANES):
        sbuf[sft - 1] = ubuf[sft:sft + sbuf.shape[1], :]
    lng = lng_ref[...]
    lnb = lnb_ref[...]
    nsub = CONV_BLOCK // SUBLANES
    for r in range(ts // CONV_BLOCK):
        base = r * CONV_BLOCK
        for lc in range(c // LANES):
            lanes = slice(lc * LANES, (lc + 1) * LANES)
            accs = [jnp.broadcast_to(b_ref[:, lanes], (SUBLANES, LANES))] * nsub
            for k in range(CONV_WIDTH):
                sft = (CONV_FIRST + k) % SUBLANES
                row = base + (CONV_FIRST + k) - sft
                wk = jnp.broadcast_to(w_ref[k:k + 1, lanes], (SUBLANES, LANES))
                for i in range(nsub):
                    lo = row + i * SUBLANES
                    if sft == 0:
                        tap = ubuf[lo:lo + SUBLANES, lanes]
                    else:
                        tap = sbuf[sft - 1, lo:lo + SUBLANES, lanes]
                    accs[i] = accs[i] + wk * tap
            for i in range(nsub):
                cbuf[base + i * SUBLANES:base + (i + 1) * SUBLANES, lanes] = accs[i]
        y = cbuf[base:base + CONV_BLOCK, :]
        mu = jnp.mean(y, axis=-1, keepdims=True)
        d = y - mu
        var = jnp.mean(d * d, axis=-1, keepdims=True)
        y = d * lax.rsqrt(var + EPS) * lng + lnb
        o_ref[base:base + CONV_BLOCK, :] = (y * _sigmoid(y)).astype(o_ref.dtype)


def _conformer(z, w_dw, b_dw, ln_g, ln_b, batch, seq, ts=256):
    m = z.shape[0]
    c = w_dw.shape[1]
    ts = min(ts, seq)
    nt = seq // ts
    row = lambda b, s: (b * nt + s, 0)
    vec = pl.BlockSpec((1, c), lambda b, s: (0, 0))
    return pl.pallas_call(
        functools.partial(_conformer_kernel, ts=ts),
        grid=(batch, nt),
        in_specs=[pl.BlockSpec((ts, c), row),
                  pl.BlockSpec((ts, c), lambda b, s: (b * nt + s, 1)),
                  pl.BlockSpec((CONV_WIDTH, c), lambda b, s: (0, 0)),
                  vec, vec, vec],
        out_specs=pl.BlockSpec((ts, c), row),
        out_shape=jax.ShapeDtypeStruct((m, c), BF16),
        scratch_shapes=[pltpu.VMEM((CONV_HALO + ts, c), F32),
                        pltpu.VMEM((SUBLANES - 1, ts + CONV_HALO - SUBLANES, c), F32),
                        pltpu.VMEM((ts, c), F32)],
        compiler_params=_params("arbitrary", "arbitrary"),
        name="conformer_conv",
    )(z, z, w_dw, b_dw.reshape(1, c), ln_g.reshape(1, c), ln_b.reshape(1, c))


QK_HALO = 8


def _mlstm_kernel(q_ref, k_ref, v_ref, o_ref, zg_ref, zgt_ref, cwq_ref, cwk_ref, cbq_ref, cbk_ref,
                  bcol_ref, brow_ref, ng_ref, out_ref, qbuf, kbuf, sq, sk, qc_scr, kc_scr, c_scr, n_scr, m_scr,
                  *, nh, dh):
    ci = pl.program_id(1)
    L = q_ref.shape[0]
    w = q_ref.shape[1]

    @pl.when(ci == 0)
    def _():
        qbuf[0:QK_HALO, :] = jnp.zeros((QK_HALO, w), F32)
        kbuf[0:QK_HALO, :] = jnp.zeros((QK_HALO, w), F32)
        c_scr[...] = jnp.zeros(c_scr.shape, F32)
        n_scr[...] = jnp.zeros(n_scr.shape, F32)
        m_scr[...] = jnp.zeros(m_scr.shape, F32)

    @pl.when(ci > 0)
    def _():
        qbuf[0:QK_HALO, :] = qbuf[L:L + QK_HALO, :]
        kbuf[0:QK_HALO, :] = kbuf[L:L + QK_HALO, :]

    qbuf[QK_HALO:QK_HALO + L, :] = q_ref[...]
    kbuf[QK_HALO:QK_HALO + L, :] = k_ref[...]

    def short_conv(buf, sbuf, cw_ref, cb_ref, dst, post_scale):
        last = QK_CONV_WIDTH - 1
        for j in range(last):
            off = QK_HALO - last + j
            sbuf[j] = buf[off:off + L, :]
        for lc in range(w // LANES):
            lanes = slice(lc * LANES, (lc + 1) * LANES)
            acc = cb_ref[:, lanes] + cw_ref[last:last + 1, lanes] * buf[QK_HALO:QK_HALO + L, lanes]
            for j in range(last):
                acc = acc + cw_ref[j:j + 1, lanes] * sbuf[j, :, lanes]
            dst[:, lanes] = acc * _sigmoid(acc) * post_scale

    short_conv(qbuf, sq, cwq_ref, cbq_ref, qc_scr, 1.0)
    short_conv(kbuf, sk, cwk_ref, cbk_ref, kc_scr, dh ** -0.5)

    zg = zg_ref[...] + bcol_ref[...]
    zgt = zgt_ref[...] + brow_ref[...]
    lf_c = _log_sigmoid(zg)
    lf_r = _log_sigmoid(zgt)
    row_i = lax.broadcasted_iota(jnp.int32, (L, L), 0)
    col_i = lax.broadcasted_iota(jnp.int32, (L, L), 1)
    causal = row_i >= col_i
    nt_dims = (((1,), (1,)), ((), ()))
    tn_dims = (((0,), (0,)), ((), ()))
    heads = range(nh)
    cols = [slice(h * dh, (h + 1) * dh) for h in heads]

    li_col = [zg[:, h:h + 1] for h in heads]
    li_row = [zgt[h:h + 1, :] for h in heads]
    lf_row = [lf_r[nh + h:nh + h + 1, :] for h in heads]
    b_col = [jnp.sum(jnp.where(causal, lf_row[h], 0.0), axis=1, keepdims=True) for h in heads]
    b_row = [jnp.sum(jnp.where(row_i <= col_i, lf_c[:, nh + h:nh + h + 1], 0.0), axis=0, keepdims=True)
             for h in heads]
    g = [jnp.sum(lf_row[h], axis=1, keepdims=True) for h in heads]
    m_prev = [m_scr[h:h + 1, 0:1] for h in heads]

    qf = [qc_scr[:, cols[h]] for h in heads]
    kf = [kc_scr[:, cols[h]] for h in heads]
    qb = [x.astype(BF16) for x in qf]
    kb = [x.astype(BF16) for x in kf]
    vb = [v_ref[:, cols[h]].astype(BF16) for h in heads]
    qk = [lax.dot_general(qb[h], kb[h], nt_dims, preferred_element_type=F32) for h in heads]
    c_prev = [c_scr[h] for h in heads]
    n_prev = [n_scr[h:h + 1, :] for h in heads]
    qc = [jnp.dot(qb[h], c_prev[h].astype(BF16), preferred_element_type=F32) for h in heads]

    a_col = [b_col[h] + m_prev[h] for h in heads]
    logw = [jnp.where(causal, b_col[h] - b_row[h] + li_row[h], -jnp.inf) for h in heads]
    m_q = [jnp.maximum(a_col[h], jnp.max(logw[h], axis=1, keepdims=True)) for h in heads]
    s = [qk[h] * jnp.exp(logw[h] - m_q[h]) for h in heads]
    inter = [jnp.exp(a_col[h] - m_q[h]) for h in heads]
    sv = [jnp.dot(s[h].astype(BF16), vb[h], preferred_element_type=F32) for h in heads]

    logu = [g[h] - b_col[h] + li_col[h] for h in heads]
    m_new = [jnp.maximum(g[h] + m_prev[h], jnp.max(logu[h], axis=0, keepdims=True)) for h in heads]
    decay = [jnp.exp(g[h] + m_prev[h] - m_new[h]) for h in heads]
    ku = [kf[h] * jnp.exp(logu[h] - m_new[h]) for h in heads]
    kv = [lax.dot_general(ku[h].astype(BF16), vb[h], tn_dims, preferred_element_type=F32) for h in heads]
    for h in heads:
        c_scr[h] = decay[h] * c_prev[h] + kv[h]
        n_scr[h:h + 1, :] = decay[h] * n_prev[h] + jnp.sum(ku[h], axis=0, keepdims=True)
        m_scr[h:h + 1, :] = jnp.broadcast_to(m_new[h], (1, m_scr.shape[1]))

    den = [inter[h] * jnp.sum(qf[h] * n_prev[h], axis=1, keepdims=True) + jnp.sum(s[h], axis=1, keepdims=True)
           for h in heads]
    rden = [1.0 / jnp.maximum(jnp.abs(den[h]), jnp.exp(-m_q[h])) for h in heads]
    ht = [_sigmoid(o_ref[:, cols[h]]) * ((inter[h] * qc[h] + sv[h]) * rden[h]) for h in heads]
    mu = [jnp.mean(ht[h], axis=-1, keepdims=True) for h in heads]
    dv = [ht[h] - mu[h] for h in heads]
    var = [jnp.mean(dv[h] * dv[h], axis=-1, keepdims=True) for h in heads]
    for h in heads:
        out_ref[:, cols[h]] = (dv[h] * lax.rsqrt(var[h] + EPS) * ng_ref[:, cols[h]]).astype(out_ref.dtype)


def _mlstm(z, zg, zgt, qk_w, qk_b, b_i, b_f, norm_g, batch, seq):
    m = z.shape[0]
    nh = N_MLSTM_HEADS
    w = norm_g.shape[0]
    dh = w // nh
    L = CHUNK
    nc = seq // L
    blk = lambda col: pl.BlockSpec((L, w), lambda b, c: (b * nc + c, col))
    full = lambda shape: pl.BlockSpec(shape, lambda b, c: (0,) * len(shape))
    bias = jnp.concatenate([b_i, b_f]).astype(F32)
    bcol = bias.reshape(1, 2 * nh)
    brow = jnp.broadcast_to(bias[:, None], (2 * nh, L))
    return pl.pallas_call(
        functools.partial(_mlstm_kernel, nh=nh, dh=dh),
        grid=(batch, nc),
        in_specs=[blk(2), blk(3), blk(4), blk(5),
                  pl.BlockSpec((L, 2 * nh), lambda b, c: (b * nc + c, 0)),
                  pl.BlockSpec((2 * nh, L), lambda b, c: (0, b * nc + c)),
                  full((QK_CONV_WIDTH, w)), full((QK_CONV_WIDTH, w)),
                  full((1, w)), full((1, w)),
                  full((1, 2 * nh)), full((2 * nh, L)), full((1, w))],
        out_specs=pl.BlockSpec((L, w), lambda b, c: (b * nc + c, 0)),
        out_shape=jax.ShapeDtypeStruct((m, w), BF16),
        scratch_shapes=[pltpu.VMEM((QK_HALO + L, w), F32),
                        pltpu.VMEM((QK_HALO + L, w), F32),
                        pltpu.VMEM((QK_CONV_WIDTH - 1, L, w), F32),
                        pltpu.VMEM((QK_CONV_WIDTH - 1, L, w), F32),
                        pltpu.VMEM((L, w), F32),
                        pltpu.VMEM((L, w), F32),
                        pltpu.VMEM((nh, dh, dh), F32),
                        pltpu.VMEM((SUBLANES, dh), F32),
                        pltpu.VMEM((SUBLANES, LANES), F32)],
        compiler_params=_params("arbitrary", "arbitrary"),
        name="mlstm",
    )(z, z, z, z, zg, zgt, qk_w[:, :w], qk_w[:, w:], qk_b[:w].reshape(1, w), qk_b[w:].reshape(1, w),
      bcol, brow, norm_g.reshape(1, w))


def _xattn_kernel(q_ref, k_ref, v_ref, o_ref, *, nh):
    d = q_ref.shape[1]
    dh = d // nh
    scale = dh ** -0.5
    nt_dims = (((1,), (1,)), ((), ()))
    for h in range(nh):
        lo, hi = h * dh, (h + 1) * dh
        sc = lax.dot_general(q_ref[:, lo:hi], k_ref[:, lo:hi], nt_dims,
                             preferred_element_type=F32) * scale
        mx = jnp.max(sc, axis=-1, keepdims=True)
        e = jnp.exp(sc - mx)
        p = e / jnp.sum(e, axis=-1, keepdims=True)
        o_ref[:, lo:hi] = jnp.dot(p.astype(BF16), v_ref[:, lo:hi],
                                  preferred_element_type=F32).astype(o_ref.dtype)


def _xattn(q, k, v, batch, seq, mem_len, ts=512):
    m, d = q.shape
    ts = min(ts, seq)
    nt = seq // ts
    return pl.pallas_call(
        functools.partial(_xattn_kernel, nh=N_XHEADS),
        grid=(batch, nt),
        in_specs=[pl.BlockSpec((ts, d), lambda b, s: (b * nt + s, 0)),
                  pl.BlockSpec((mem_len, d), lambda b, s: (b, 0)),
                  pl.BlockSpec((mem_len, d), lambda b, s: (b, 0))],
        out_specs=pl.BlockSpec((ts, d), lambda b, s: (b * nt + s, 0)),
        out_shape=jax.ShapeDtypeStruct((m, d), BF16),
        compiler_params=_params("arbitrary", "arbitrary"),
        name="cross_attn",
    )(q, k, v)


def _swiglu_tile(h_ref, wgb_ref, wub_ref, o_ref):
    h = h_ref[...]
    g = jnp.dot(h, wgb_ref[...], preferred_element_type=F32)
    u = jnp.dot(h, wub_ref[...], preferred_element_type=F32)
    o_ref[...] = (g * _sigmoid(g) * u).astype(o_ref.dtype)


def _gateup_kernel(h_ref, wg_ref, wu_ref, o_ref, wgb_ref, wub_ref):
    @pl.when(pl.program_id(1) == 0)
    def _():
        wgb_ref[...] = wg_ref[...].astype(BF16)
        wub_ref[...] = wu_ref[...].astype(BF16)

    _swiglu_tile(h_ref, wgb_ref, wub_ref, o_ref)


def _gateup(h, wg, wu, w_lead, tm=1024, tc=512):
    m, d = h.shape
    f = wg.shape[-1]
    tm = min(tm, m)
    nlead = len(w_lead)
    wspec = pl.BlockSpec((None,) * nlead + (d, tc), lambda c, i: tuple(w_lead) + (0, c))
    return pl.pallas_call(
        _gateup_kernel,
        grid=(f // tc, m // tm),
        in_specs=[pl.BlockSpec((tm, d), lambda c, i: (i, 0)), wspec, wspec],
        out_specs=pl.BlockSpec((tm, tc), lambda c, i: (i, c)),
        out_shape=jax.ShapeDtypeStruct((m, f), BF16),
        scratch_shapes=[pltpu.VMEM((d, tc), BF16), pltpu.VMEM((d, tc), BF16)],
        compiler_params=_params("arbitrary", "arbitrary"),
        name="swiglu_gateup",
    )(h, wg, wu)


def _expert_changed(te_ref, i):
    return jnp.logical_or(i == 0, te_ref[i] != te_ref[jnp.maximum(i - 1, 0)])


def _moe_gateup_kernel(te_ref, ta_ref, h_ref, wg_ref, wu_ref, o_ref, wgb_ref, wub_ref):
    i = pl.program_id(1)

    @pl.when(_expert_changed(te_ref, i))
    def _():
        wgb_ref[...] = wg_ref[...].astype(BF16)
        wub_ref[...] = wu_ref[...].astype(BF16)

    @pl.when(ta_ref[i] > 0)
    def _():
        _swiglu_tile(h_ref, wgb_ref, wub_ref, o_ref)

    @pl.when(ta_ref[i] == 0)
    def _():
        o_ref[...] = jnp.zeros(o_ref.shape, o_ref.dtype)


def _moe_gateup(tile_e, tile_on, hs, wg, wu, layer, tc=512):
    r, d = hs.shape
    f = wg.shape[-1]
    tg = MOE_TILE
    wspec = pl.BlockSpec((None, None, d, tc), lambda c, i, te, ta: (layer, te[i], 0, c))
    return pl.pallas_call(
        _moe_gateup_kernel,
        grid_spec=pltpu.PrefetchScalarGridSpec(
            num_scalar_prefetch=2,
            grid=(f // tc, r // tg),
            in_specs=[pl.BlockSpec((tg, d), lambda c, i, te, ta: (i, 0)), wspec, wspec],
            out_specs=pl.BlockSpec((tg, tc), lambda c, i, te, ta: (i, c)),
            scratch_shapes=[pltpu.VMEM((d, tc), BF16), pltpu.VMEM((d, tc), BF16)]),
        out_shape=jax.ShapeDtypeStruct((r, f), BF16),
        compiler_params=_params("arbitrary", "arbitrary"),
        name="moe_gateup",
    )(tile_e, tile_on, hs, wg, wu)


def _moe_down_kernel(te_ref, ta_ref, a_ref, w_ref, o_ref, wb_ref):
    i = pl.program_id(1)

    @pl.when(_expert_changed(te_ref, i))
    def _():
        wb_ref[...] = w_ref[...].astype(BF16)

    @pl.when(ta_ref[i] > 0)
    def _():
        o_ref[...] = jnp.dot(a_ref[...], wb_ref[...], preferred_element_type=F32)

    @pl.when(ta_ref[i] == 0)
    def _():
        o_ref[...] = jnp.zeros(o_ref.shape, o_ref.dtype)


def _moe_down(tile_e, tile_on, a, wd, layer, tn=512):
    r, f = a.shape
    d = wd.shape[-1]
    tg = MOE_TILE
    return pl.pallas_call(
        _moe_down_kernel,
        grid_spec=pltpu.PrefetchScalarGridSpec(
            num_scalar_prefetch=2,
            grid=(d // tn, r // tg),
            in_specs=[pl.BlockSpec((tg, f), lambda j, i, te, ta: (i, 0)),
                      pl.BlockSpec((None, None, f, tn), lambda j, i, te, ta: (layer, te[i], 0, j))],
            out_specs=pl.BlockSpec((tg, tn), lambda j, i, te, ta: (i, j)),
            scratch_shapes=[pltpu.VMEM((f, tn), BF16)]),
        out_shape=jax.ShapeDtypeStruct((r, d), F32),
        compiler_params=_params("arbitrary", "arbitrary"),
        name="moe_down",
    )(tile_e, tile_on, a, wd)


def _router_kernel(h_ref, wr_ref, br_ref, o_ref, *, ne):
    logits = jnp.dot(h_ref[...], wr_ref[...].astype(BF16), preferred_element_type=F32) + br_ref[...]
    lane = lax.broadcasted_iota(jnp.int32, logits.shape, 1).astype(F32)
    lg = jnp.where(lane < ne, logits, -jnp.inf)
    v1 = jnp.max(lg, axis=1, keepdims=True)
    i1 = jnp.min(jnp.where(lg == v1, lane, float(LANES)), axis=1, keepdims=True)
    lg2 = jnp.where(lane == i1, -jnp.inf, lg)
    v2 = jnp.max(lg2, axis=1, keepdims=True)
    i2 = jnp.min(jnp.where(lg2 == v2, lane, float(LANES)), axis=1, keepdims=True)
    e2 = jnp.exp(v2 - v1)
    w1 = 1.0 / (1.0 + e2)
    w2 = e2 / (1.0 + e2)
    o_ref[...] = jnp.where(lane == 0, i1, jnp.where(lane == 1, i2,
                           jnp.where(lane == 2, w1, jnp.where(lane == 3, w2, 0.0))))


def _router(h, w_r, b_r, tm=1024):
    m, d = h.shape
    ne = w_r.shape[1]
    tm = min(tm, m)
    wr = jnp.zeros((d, LANES), F32).at[:, :ne].set(w_r)
    br = jnp.zeros((1, LANES), F32).at[0, :ne].set(b_r)
    return pl.pallas_call(
        functools.partial(_router_kernel, ne=ne),
        grid=(m // tm,),
        in_specs=[pl.BlockSpec((tm, d), lambda i: (i, 0)),
                  pl.BlockSpec((d, LANES), lambda i: (0, 0)),
                  pl.BlockSpec((1, LANES), lambda i: (0, 0))],
        out_specs=pl.BlockSpec((tm, LANES), lambda i: (i, 0)),
        out_shape=jax.ShapeDtypeStruct((m, LANES), F32),
        compiler_params=_params("arbitrary"),
        name="moe_router",
    )(h, wr, br)


def _row_copy(src_hbm, row, buf, slot, sem):
    return pltpu.make_async_copy(src_hbm.at[pl.ds(row, 1), :], buf.at[pl.ds(slot, 1), :], sem)


def _dispatch_kernel(src_ref, nrows_ref, x_hbm, g_ref, o_ref, buf, sem, *, rows):
    base = pl.program_id(0) * rows
    active = base < nrows_ref[0]

    @pl.when(active)
    def _():
        def start(c, carry):
            for u in range(DMA_UNROLL):
                r = u * (rows // DMA_UNROLL) + c
                _row_copy(x_hbm, src_ref[base + r], buf, r, sem).start(priority=u % 2)
            return carry

        def wait(c, carry):
            for u in range(DMA_UNROLL):
                _row_copy(x_hbm, 0, buf, c * DMA_UNROLL + u, sem).wait()
            return carry

        lax.fori_loop(0, rows // DMA_UNROLL, start, 0)
        lax.fori_loop(0, rows // DMA_UNROLL, wait, 0)
        x = buf[...]
        ms = jnp.mean(x * x, axis=-1, keepdims=True)
        o_ref[...] = (x * lax.rsqrt(ms + EPS) * g_ref[...]).astype(o_ref.dtype)

    @pl.when(jnp.logical_not(active))
    def _():
        o_ref[...] = jnp.zeros(o_ref.shape, o_ref.dtype)


def _dispatch(src, n_used, x, g, n_rows):
    m, d = x.shape
    rows = GATHER_ROWS
    return pl.pallas_call(
        functools.partial(_dispatch_kernel, rows=rows),
        grid_spec=pltpu.PrefetchScalarGridSpec(
            num_scalar_prefetch=2,
            grid=(n_rows // rows,),
            in_specs=[pl.BlockSpec(memory_space=pl.ANY),
                      pl.BlockSpec((1, d), lambda i, s, n: (0, 0))],
            out_specs=pl.BlockSpec((rows, d), lambda i, s, n: (i, 0)),
            scratch_shapes=[pltpu.VMEM((rows, d), F32), pltpu.SemaphoreType.DMA(())]),
        out_shape=jax.ShapeDtypeStruct((n_rows, d), BF16),
        compiler_params=_params("arbitrary"),
        name="moe_dispatch",
    )(src, n_used, x, g.reshape(1, d))


def _combine_kernel(p1_ref, p2_ref, x_ref, route_ref, g_ref, y_hbm, *rest, rows, emit_x):
    if emit_x:
        o_ref, hn_ref, buf1, buf2, sem = rest
    else:
        hn_ref, buf1, buf2, sem = rest
    base = pl.program_id(0) * rows

    def start(c, carry):
        for u in range(DMA_UNROLL):
            r = c * DMA_UNROLL + u
            _row_copy(y_hbm, p1_ref[base + r], buf1, r, sem.at[0]).start(priority=0)
            _row_copy(y_hbm, p2_ref[base + r], buf2, r, sem.at[1]).start(priority=1)
        return carry

    def wait(c, carry):
        for u in range(DMA_UNROLL):
            r = c * DMA_UNROLL + u
            _row_copy(y_hbm, 0, buf1, r, sem.at[0]).wait()
            _row_copy(y_hbm, 0, buf2, r, sem.at[1]).wait()
        return carry

    lax.fori_loop(0, rows // DMA_UNROLL, start, 0)
    lax.fori_loop(0, rows // DMA_UNROLL, wait, 0)
    route = route_ref[...]
    out = x_ref[...] + route[:, 2:3] * buf1[...] + route[:, 3:4] * buf2[...]
    if emit_x:
        o_ref[...] = out
    ms = jnp.mean(out * out, axis=-1, keepdims=True)
    hn_ref[...] = (out * lax.rsqrt(ms + EPS) * g_ref[...]).astype(hn_ref.dtype)


def _combine(p1, p2, x, route, y, next_gain, next_dtype, emit_x):
    m, d = x.shape
    rows = GATHER_ROWS
    tile = pl.BlockSpec((rows, d), lambda i, a, b: (i, 0))
    out_specs = [tile, tile] if emit_x else [tile]
    out_shape = [jax.ShapeDtypeStruct((m, d), next_dtype)]
    if emit_x:
        out_shape.insert(0, jax.ShapeDtypeStruct((m, d), F32))
    res = pl.pallas_call(
        functools.partial(_combine_kernel, rows=rows, emit_x=emit_x),
        grid_spec=pltpu.PrefetchScalarGridSpec(
            num_scalar_prefetch=2,
            grid=(m // rows,),
            in_specs=[tile,
                      pl.BlockSpec((rows, LANES), lambda i, a, b: (i, 0)),
                      pl.BlockSpec((1, d), lambda i, a, b: (0, 0)),
                      pl.BlockSpec(memory_space=pl.ANY)],
            out_specs=out_specs,
            scratch_shapes=[pltpu.VMEM((rows, d), F32), pltpu.VMEM((rows, d), F32),
                            pltpu.SemaphoreType.DMA((2,))]),
        out_shape=out_shape,
        compiler_params=_params("arbitrary"),
        name="moe_combine",
    )(p1, p2, x, route, next_gain.reshape(1, d), y)
    return (res[0], res[1]) if emit_x else (None, res[0])


def _moe_plan(route, m, ne, tg):
    i1 = route[:, 0].astype(jnp.int32)
    i2 = route[:, 1].astype(jnp.int32)
    experts = jnp.arange(ne, dtype=jnp.int32)
    sel = ((i1[:, None] == experts) | (i2[:, None] == experts)).astype(jnp.int32)
    counts = jnp.sum(sel, axis=0)
    padded = ((counts + tg - 1) // tg) * tg
    ends = jnp.cumsum(padded)
    pos = (ends - padded)[None, :] + jnp.cumsum(sel, axis=0) - sel
    p1 = jnp.take_along_axis(pos, i1[:, None], axis=1)[:, 0]
    p2 = jnp.take_along_axis(pos, i2[:, None], axis=1)[:, 0]
    n_tiles = (2 * m) // tg + ne
    tok = jnp.arange(m, dtype=jnp.int32)
    src = jnp.zeros((n_tiles * tg,), jnp.int32).at[p1].set(tok).at[p2].set(tok)
    tile_start = jnp.arange(n_tiles, dtype=jnp.int32) * tg
    tile_e = jnp.minimum(jnp.sum((tile_start[:, None] >= ends[None, :]).astype(jnp.int32), axis=1), ne - 1)
    tile_on = (tile_start < ends[-1]).astype(jnp.int32)
    return p1, p2, src, ends[-1:], tile_e, tile_on, n_tiles * tg


def _moe_block(x, h, g_ffn, w_r, b_r, wg, wu, wd, layer, next_gain, next_dtype, emit_x):
    m, d = x.shape
    route = _router(h, w_r, b_r)
    p1, p2, src, n_used, tile_e, tile_on, n_rows = _moe_plan(route, m, N_EXPERTS, MOE_TILE)
    hs = _dispatch(src, n_used, x, g_ffn, n_rows)
    a = _moe_gateup(tile_e, tile_on, hs, wg, wu, layer)
    y = _moe_down(tile_e, tile_on, a, wd, layer)
    return _combine(p1, p2, x, route, y, next_gain, next_dtype, emit_x)


def kernel(x, mem, norm_mix, w_in, conv_dw_w, conv_dw_b, conv_ln_g, conv_ln_b, qk_conv_w, qk_conv_b, b_igate, b_fgate, mlstm_norm_g, w_out, norm_cross, norm_mem, w_cq, w_ck, w_cv, w_co, norm_ffn, w_gate_dense, w_up_dense, w_down_dense, w_router, b_router, w_gate_moe, w_up_moe, w_down_moe, norm_final):
    batch, seq, d = x.shape
    mem_len = mem.shape[1]
    depth = norm_mix.shape[0]
    d_conv = conv_dw_w.shape[2]
    d_mlstm = mlstm_norm_g.shape[1]
    n_main = 2 * d_conv + 4 * d_mlstm
    nh = N_MLSTM_HEADS
    xf = x.reshape(batch * seq, d)
    memf = mem.reshape(batch * mem_len, d)

    w_in_nk = jnp.swapaxes(w_in, 1, 2)
    h = _rmsnorm(xf, norm_mix[0], BF16)
    for l in range(depth):
        last = l == depth - 1
        next_gain = norm_final if last else norm_mix[l + 1]
        next_dtype = F32 if last else BF16
        z = _matmul([h], w_in_nk, (l,), n_cols=n_main, w_is_nk=True, tn=1024, name="w_in")
        zgt = _gates(h, w_in_nk, l, n_main, 2 * nh)
        yc = _conformer(z, conv_dw_w[l], conv_dw_b[l], conv_ln_g[l], conv_ln_b[l], batch, seq)
        ym = _mlstm(z, zgt.T, zgt, qk_conv_w[l], qk_conv_b[l], b_igate[l], b_fgate[l], mlstm_norm_g[l],
                    batch, seq)
        xf, h = _matmul([yc, ym], w_out, (l,), residual=xf, norm_gain=norm_cross[l], tm=512, tn=d,
                        name="w_out")
        mem_n = _rmsnorm(memf, norm_mem[l], BF16)
        q = _matmul([h], w_cq, (l,), out_dtype=BF16, tn=1024, name="w_cq")
        kk = _matmul([mem_n], w_ck, (l,), out_dtype=BF16, name="w_ck")
        vv = _matmul([mem_n], w_cv, (l,), out_dtype=BF16, name="w_cv")
        att = _xattn(q, kk, vv, batch, seq, mem_len)
        xf, h = _matmul([att], w_co, (l,), residual=xf, norm_gain=norm_ffn[l], tm=512, tn=d, name="w_co")
        j = l // 2
        if l % 2 == 0:
            a = _gateup(h, w_gate_dense, w_up_dense, (j,))
            xf = _matmul([a], w_down_dense, (j,), residual=xf, tm=512, tn=512, name="w_down")
            h = _rmsnorm(xf, next_gain, next_dtype)
        else:
            xf, h = _moe_block(xf, h, norm_ffn[l], w_router[j], b_router[j], w_gate_moe, w_up_moe,
                               w_down_moe, j, next_gain, next_dtype, emit_x=not last)
    return h.reshape(batch, seq, d)
```

```python
import functools

import jax
import jax.numpy as jnp
from jax import lax
from jax.experimental import pallas as pl
from jax.experimental.pallas import tpu as pltpu

F32 = jnp.float32
BF16 = jnp.bfloat16
EPS = 1e-6

V7X_VMEM_BYTES = 64 * 1024 * 1024
VMEM_LIMIT = V7X_VMEM_BYTES - 8 * 1024 * 1024
LANES = 128
SUBLANES = 8

N_MLSTM_HEADS = 4
N_XHEADS = 4
CONV_WIDTH = 31
QK_CONV_WIDTH = 4
CHUNK = 128
N_EXPERTS = 8
MOE_TILE = 512
GATHER_ROWS = 512
DMA_UNROLL = 8


def _params(*sem):
    return pltpu.CompilerParams(dimension_semantics=sem, vmem_limit_bytes=VMEM_LIMIT)


def _sigmoid(x):
    return 1.0 / (1.0 + jnp.exp(-x))


def _log_sigmoid(x):
    return jnp.minimum(x, 0.0) - jnp.log(1.0 + jnp.exp(-jnp.abs(x)))


def _rmsnorm_kernel(x_ref, g_ref, o_ref):
    x = x_ref[...]
    ms = jnp.mean(x * x, axis=-1, keepdims=True)
    o_ref[...] = (x * lax.rsqrt(ms + EPS) * g_ref[...]).astype(o_ref.dtype)


def _rmsnorm(x, g, out_dtype, tm=512):
    m, d = x.shape
    tm = min(tm, m)
    return pl.pallas_call(
        _rmsnorm_kernel,
        grid=(m // tm,),
        in_specs=[pl.BlockSpec((tm, d), lambda i: (i, 0)),
                  pl.BlockSpec((1, d), lambda i: (0, 0))],
        out_specs=pl.BlockSpec((tm, d), lambda i: (i, 0)),
        out_shape=jax.ShapeDtypeStruct((m, d), out_dtype),
        compiler_params=_params("arbitrary"),
        name="rmsnorm",
    )(x, g.reshape(1, d))


NT_DIMS = (((1,), (1,)), ((), ()))


def _matmul_kernel(*refs, k_sizes, w_is_nk, has_res, has_norm):
    na = len(k_sizes)
    a_refs = refs[:na]
    w_ref = refs[na]
    pos = na + 1
    r_ref = g_ref = hn_ref = None
    if has_res:
        r_ref = refs[pos]
        pos += 1
    if has_norm:
        g_ref = refs[pos]
        pos += 1
    o_ref = refs[pos]
    if has_norm:
        hn_ref = refs[pos + 1]
    wb_ref = refs[-1]

    @pl.when(pl.program_id(1) == 0)
    def _():
        wb_ref[...] = w_ref[...].astype(BF16)

    acc = None
    k0 = 0
    for a_ref, ks in zip(a_refs, k_sizes):
        if w_is_nk:
            part = lax.dot_general(a_ref[...], wb_ref[:, k0:k0 + ks], NT_DIMS, preferred_element_type=F32)
        else:
            part = jnp.dot(a_ref[...], wb_ref[k0:k0 + ks, :], preferred_element_type=F32)
        acc = part if acc is None else acc + part
        k0 += ks
    if has_res:
        acc = acc + r_ref[...]
    o_ref[...] = acc.astype(o_ref.dtype)
    if has_norm:
        ms = jnp.mean(acc * acc, axis=-1, keepdims=True)
        hn_ref[...] = (acc * lax.rsqrt(ms + EPS) * g_ref[...]).astype(hn_ref.dtype)


def _matmul(a_list, w, w_lead=(), *, n_cols=None, w_is_nk=False, residual=None, norm_gain=None,
            out_dtype=F32, tm=1024, tn=512, name="matmul"):
    m = a_list[0].shape[0]
    k_sizes = tuple(a.shape[1] for a in a_list)
    k = sum(k_sizes)
    n_total = w.shape[-2] if w_is_nk else w.shape[-1]
    n = n_total if n_cols is None else n_cols
    tm = min(tm, m)
    tn = min(tn, n)
    nlead = len(w_lead)
    lead = tuple(w_lead)
    if w_is_nk:
        w_block, w_map, wb_shape = (None,) * nlead + (tn, k), (lambda j, i: lead + (j, 0)), (tn, k)
    else:
        w_block, w_map, wb_shape = (None,) * nlead + (k, tn), (lambda j, i: lead + (0, j)), (k, tn)
    w_mode = dict(pipeline_mode=pl.Buffered(1)) if n == tn else {}
    in_specs = [pl.BlockSpec((tm, ks), lambda j, i: (i, 0)) for ks in k_sizes]
    in_specs.append(pl.BlockSpec(w_block, w_map, **w_mode))
    args = list(a_list) + [w]
    out_tile = pl.BlockSpec((tm, tn), lambda j, i: (i, j))
    if residual is not None:
        in_specs.append(out_tile)
        args.append(residual)
    out_specs, out_shape = out_tile, jax.ShapeDtypeStruct((m, n), out_dtype)
    if norm_gain is not None:
        assert tn == n, "the fused RMSNorm needs whole rows"
        in_specs.append(pl.BlockSpec((1, n), lambda j, i: (0, 0)))
        args.append(norm_gain.reshape(1, n))
        out_specs, out_shape = [out_tile, out_tile], [out_shape, jax.ShapeDtypeStruct((m, n), BF16)]
    return pl.pallas_call(
        functools.partial(_matmul_kernel, k_sizes=k_sizes, w_is_nk=w_is_nk,
                          has_res=residual is not None, has_norm=norm_gain is not None),
        grid=(n // tn, m // tm),
        in_specs=in_specs,
        out_specs=out_specs,
        out_shape=out_shape,
        scratch_shapes=[pltpu.VMEM(wb_shape, BF16)],
        compiler_params=_params("arbitrary", "arbitrary"),
        name=name,
    )(*args)


def _gates_kernel(h_ref, w_ref, o_ref):
    o_ref[...] = lax.dot_general(w_ref[...].astype(BF16), h_ref[...], NT_DIMS, preferred_element_type=F32)


def _gates(h, w_nk, lead, row0, nrows, tm=1024):
    m, k = h.shape
    tm = min(tm, m)
    assert row0 % nrows == 0
    return pl.pallas_call(
        _gates_kernel,
        grid=(m // tm,),
        in_specs=[pl.BlockSpec((tm, k), lambda i: (i, 0)),
                  pl.BlockSpec((None, nrows, k), lambda i: (lead, row0 // nrows, 0))],
        out_specs=pl.BlockSpec((nrows, tm), lambda i: (0, i)),
        out_shape=jax.ShapeDtypeStruct((nrows, m), F32),
        compiler_params=_params("arbitrary"),
        name="w_in_gates",
    )(h, w_nk)


CONV_HALO = 32
CONV_BLOCK = 64
CONV_FIRST = CONV_HALO - (CONV_WIDTH - 1)


def _conformer_kernel(a_ref, g_ref, w_ref, b_ref, lng_ref, lnb_ref, o_ref, ubuf, sbuf, cbuf, *, ts):
    s = pl.program_id(1)
    c = a_ref.shape[1]

    @pl.when(s == 0)
    def _():
        ubuf[0:CONV_HALO, :] = jnp.zeros((CONV_HALO, c), F32)

    @pl.when(s > 0)
    def _():
        ubuf[0:CONV_HALO, :] = ubuf[ts:ts + CONV_HALO, :]

    ubuf[CONV_HALO:CONV_HALO + ts, :] = a_ref[...] * _sigmoid(g_ref[...])
    for sft in range(1, SUBLANES):
        sbuf[sft - 1] = ubuf[sft:sft + sbuf.shape[1], :]
    lng = lng_ref[...]
    lnb = lnb_ref[...]
    nsub = CONV_BLOCK // SUBLANES
    for r in range(ts // CONV_BLOCK):
        base = r * CONV_BLOCK
        for lc in range(c // LANES):
            lanes = slice(lc * LANES, (lc + 1) * LANES)
            accs = [jnp.broadcast_to(b_ref[:, lanes], (SUBLANES, LANES))] * nsub
            for k in range(CONV_WIDTH):
                sft = (CONV_FIRST + k) % SUBLANES
                row = base + (CONV_FIRST + k) - sft
                wk = jnp.broadcast_to(w_ref[k:k + 1, lanes], (SUBLANES, LANES))
                for i in range(nsub):
                    lo = row + i * SUBLANES
                    if sft == 0:
                        tap = ubuf[lo:lo + SUBLANES, lanes]
                    else:
                        tap = sbuf[sft - 1, lo:lo + SUBLANES, lanes]
                    accs[i] = accs[i] + wk * tap
            for i in range(nsub):
                cbuf[base + i * SUBLANES:base + (i + 1) * SUBLANES, lanes] = accs[i]
        y = cbuf[base:base + CONV_BLOCK, :]
        mu = jnp.mean(y, axis=-1, keepdims=True)
        d = y - mu
        var = jnp.mean(d * d, axis=-1, keepdims=True)
        y = d * lax.rsqrt(var + EPS) * lng + lnb
        o_ref[base:base + CONV_BLOCK, :] = (y * _sigmoid(y)).astype(o_ref.dtype)


def _conformer(z, w_dw, b_dw, ln_g, ln_b, batch, seq, ts=256):
    m = z.shape[0]
    c = w_dw.shape[1]
    ts = min(ts, seq)
    nt = seq // ts
    row = lambda b, s: (b * nt + s, 0)
    vec = pl.BlockSpec((1, c), lambda b, s: (0, 0))
    return pl.pallas_call(
        functools.partial(_conformer_kernel, ts=ts),
        grid=(batch, nt),
        in_specs=[pl.BlockSpec((ts, c), row),
                  pl.BlockSpec((ts, c), lambda b, s: (b * nt + s, 1)),
                  pl.BlockSpec((CONV_WIDTH, c), lambda b, s: (0, 0)),
                  vec, vec, vec],
        out_specs=pl.BlockSpec((ts, c), row),
        out_shape=jax.ShapeDtypeStruct((m, c), BF16),
        scratch_shapes=[pltpu.VMEM((CONV_HALO + ts, c), F32),
                        pltpu.VMEM((SUBLANES - 1, ts + CONV_HALO - SUBLANES, c), F32),
                        pltpu.VMEM((ts, c), F32)],
        compiler_params=_params("arbitrary", "arbitrary"),
        name="conformer_conv",
    )(z, z, w_dw, b_dw.reshape(1, c), ln_g.reshape(1, c), ln_b.reshape(1, c))


QK_HALO = 8


def _mlstm_kernel(q_ref, k_ref, v_ref, o_ref, zg_ref, zgt_ref, cwq_ref, cwk_ref, cbq_ref, cbk_ref,
                  bcol_ref, brow_ref, ng_ref, out_ref, qbuf, kbuf, sq, sk, qc_scr, kc_scr, c_scr, n_scr, m_scr,
                  *, nh, dh):
    ci = pl.program_id(1)
    L = q_ref.shape[0]
    w = q_ref.shape[1]

    @pl.when(ci == 0)
    def _():
        qbuf[0:QK_HALO, :] = jnp.zeros((QK_HALO, w), F32)
        kbuf[0:QK_HALO, :] = jnp.zeros((QK_HALO, w), F32)
        c_scr[...] = jnp.zeros(c_scr.shape, F32)
        n_scr[...] = jnp.zeros(n_scr.shape, F32)
        m_scr[...] = jnp.zeros(m_scr.shape, F32)

    @pl.when(ci > 0)
    def _():
        qbuf[0:QK_HALO, :] = qbuf[L:L + QK_HALO, :]
        kbuf[0:QK_HALO, :] = kbuf[L:L + QK_HALO, :]

    qbuf[QK_HALO:QK_HALO + L, :] = q_ref[...]
    kbuf[QK_HALO:QK_HALO + L, :] = k_ref[...]

    def short_conv(buf, sbuf, cw_ref, cb_ref, dst, post_scale):
        last = QK_CONV_WIDTH - 1
        for j in range(last):
            off = QK_HALO - last + j
            sbuf[j] = buf[off:off + L, :]
        for lc in range(w // LANES):
            lanes = slice(lc * LANES, (lc + 1) * LANES)
            acc = cb_ref[:, lanes] + cw_ref[last:last + 1, lanes] * buf[QK_HALO:QK_HALO + L, lanes]
            for j in range(last):
                acc = acc + cw_ref[j:j + 1, lanes] * sbuf[j, :, lanes]
            dst[:, lanes] = acc * _sigmoid(acc) * post_scale

    short_conv(qbuf, sq, cwq_ref, cbq_ref, qc_scr, 1.0)
    short_conv(kbuf, sk, cwk_ref, cbk_ref, kc_scr, dh ** -0.5)

    zg = zg_ref[...] + bcol_ref[...]
    zgt = zgt_ref[...] + brow_ref[...]
    lf_c = _log_sigmoid(zg)
    lf_r = _log_sigmoid(zgt)
    row_i = lax.broadcasted_iota(jnp.int32, (L, L), 0)
    col_i = lax.broadcasted_iota(jnp.int32, (L, L), 1)
    causal = row_i >= col_i
    nt_dims = (((1,), (1,)), ((), ()))
    tn_dims = (((0,), (0,)), ((), ()))
    heads = range(nh)
    cols = [slice(h * dh, (h + 1) * dh) for h in heads]

    li_col = [zg[:, h:h + 1] for h in heads]
    li_row = [zgt[h:h + 1, :] for h in heads]
    lf_row = [lf_r[nh + h:nh + h + 1, :] for h in heads]
    b_col = [jnp.sum(jnp.where(causal, lf_row[h], 0.0), axis=1, keepdims=True) for h in heads]
    b_row = [jnp.sum(jnp.where(row_i <= col_i, lf_c[:, nh + h:nh + h + 1], 0.0), axis=0, keepdims=True)
             for h in heads]
    g = [jnp.sum(lf_row[h], axis=1, keepdims=True) for h in heads]
    m_prev = [m_scr[h:h + 1, 0:1] for h in heads]

    qf = [qc_scr[:, cols[h]] for h in heads]
    kf = [kc_scr[:, cols[h]] for h in heads]
    qb = [x.astype(BF16) for x in qf]
    kb = [x.astype(BF16) for x in kf]
    vb = [v_ref[:, cols[h]].astype(BF16) for h in heads]
    qk = [lax.dot_general(qb[h], kb[h], nt_dims, preferred_element_type=F32) for h in heads]
    c_prev = [c_scr[h] for h in heads]
    n_prev = [n_scr[h:h + 1, :] for h in heads]
    qc = [jnp.dot(qb[h], c_prev[h].astype(BF16), preferred_element_type=F32) for h in heads]

    a_col = [b_col[h] + m_prev[h] for h in heads]
    logw = [jnp.where(causal, b_col[h] - b_row[h] + li_row[h], -jnp.inf) for h in heads]
    m_q = [jnp.maximum(a_col[h], jnp.max(logw[h], axis=1, keepdims=True)) for h in heads]
    s = [qk[h] * jnp.exp(logw[h] - m_q[h]) for h in heads]
    inter = [jnp.exp(a_col[h] - m_q[h]) for h in heads]
    sv = [jnp.dot(s[h].astype(BF16), vb[h], preferred_element_type=F32) for h in heads]

    logu = [g[h] - b_col[h] + li_col[h] for h in heads]
    m_new = [jnp.maximum(g[h] + m_prev[h], jnp.max(logu[h], axis=0, keepdims=True)) for h in heads]
    decay = [jnp.exp(g[h] + m_prev[h] - m_new[h]) for h in heads]
    ku = [kf[h] * jnp.exp(logu[h] - m_new[h]) for h in heads]
    kv = [lax.dot_general(ku[h].astype(BF16), vb[h], tn_dims, preferred_element_type=F32) for h in heads]
    for h in heads:
        c_scr[h] = decay[h] * c_prev[h] + kv[h]
        n_scr[h:h + 1, :] = decay[h] * n_prev[h] + jnp.sum(ku[h], axis=0, keepdims=True)
        m_scr[h:h + 1, :] = jnp.broadcast_to(m_new[h], (1, m_scr.shape[1]))

    den = [inter[h] * jnp.sum(qf[h] * n_prev[h], axis=1, keepdims=True) + jnp.sum(s[h], axis=1, keepdims=True)
           for h in heads]
    rden = [1.0 / jnp.maximum(jnp.abs(den[h]), jnp.exp(-m_q[h])) for h in heads]
    ht = [_sigmoid(o_ref[:, cols[h]]) * ((inter[h] * qc[h] + sv[h]) * rden[h]) for h in heads]
    mu = [jnp.mean(ht[h], axis=-1, keepdims=True) for h in heads]
    dv = [ht[h] - mu[h] for h in heads]
    var = [jnp.mean(dv[h] * dv[h], axis=-1, keepdims=True) for h in heads]
    for h in heads:
        out_ref[:, cols[h]] = (dv[h] * lax.rsqrt(var[h] + EPS) * ng_ref[:, cols[h]]).astype(out_ref.dtype)


def _mlstm(z, zg, zgt, qk_w, qk_b, b_i, b_f, norm_g, batch, seq):
    m = z.shape[0]
    nh = N_MLSTM_HEADS
    w = norm_g.shape[0]
    dh = w // nh
    L = CHUNK
    nc = seq // L
    blk = lambda col: pl.BlockSpec((L, w), lambda b, c: (b * nc + c, col))
    full = lambda shape: pl.BlockSpec(shape, lambda b, c: (0,) * len(shape))
    bias = jnp.concatenate([b_i, b_f]).astype(F32)
    bcol = bias.reshape(1, 2 * nh)
    brow = jnp.broadcast_to(bias[:, None], (2 * nh, L))
    return pl.pallas_call(
        functools.partial(_mlstm_kernel, nh=nh, dh=dh),
        grid=(batch, nc),
        in_specs=[blk(2), blk(3), blk(4), blk(5),
                  pl.BlockSpec((L, 2 * nh), lambda b, c: (b * nc + c, 0)),
                  pl.BlockSpec((2 * nh, L), lambda b, c: (0, b * nc + c)),
                  full((QK_CONV_WIDTH, w)), full((QK_CONV_WIDTH, w)),
                  full((1, w)), full((1, w)),
                  full((1, 2 * nh)), full((2 * nh, L)), full((1, w))],
        out_specs=pl.BlockSpec((L, w), lambda b, c: (b * nc + c, 0)),
        out_shape=jax.ShapeDtypeStruct((m, w), BF16),
        scratch_shapes=[pltpu.VMEM((QK_HALO + L, w), F32),
                        pltpu.VMEM((QK_HALO + L, w), F32),
                        pltpu.VMEM((QK_CONV_WIDTH - 1, L, w), F32),
                        pltpu.VMEM((QK_CONV_WIDTH - 1, L, w), F32),
                        pltpu.VMEM((L, w), F32),
                        pltpu.VMEM((L, w), F32),
                        pltpu.VMEM((nh, dh, dh), F32),
                        pltpu.VMEM((SUBLANES, dh), F32),
                        pltpu.VMEM((SUBLANES, LANES), F32)],
        compiler_params=_params("arbitrary", "arbitrary"),
        name="mlstm",
    )(z, z, z, z, zg, zgt, qk_w[:, :w], qk_w[:, w:], qk_b[:w].reshape(1, w), qk_b[w:].reshape(1, w),
      bcol, brow, norm_g.reshape(1, w))


def _xattn_kernel(q_ref, k_ref, v_ref, o_ref, *, nh):
    d = q_ref.shape[1]
    dh = d // nh
    scale = dh ** -0.5
    nt_dims = (((1,), (1,)), ((), ()))
    for h in range(nh):
        lo, hi = h * dh, (h + 1) * dh
        sc = lax.dot_general(q_ref[:, lo:hi], k_ref[:, lo:hi], nt_dims,
                             preferred_element_type=F32) * scale
        mx = jnp.max(sc, axis=-1, keepdims=True)
        e = jnp.exp(sc - mx)
        p = e / jnp.sum(e, axis=-1, keepdims=True)
        o_ref[:, lo:hi] = jnp.dot(p.astype(BF16), v_ref[:, lo:hi],
                                  preferred_element_type=F32).astype(o_ref.dtype)


def _xattn(q, k, v, batch, seq, mem_len, ts=512):
    m, d = q.shape
    ts = min(ts, seq)
    nt = seq // ts
    return pl.pallas_call(
        functools.partial(_xattn_kernel, nh=N_XHEADS),
        grid=(batch, nt),
        in_specs=[pl.BlockSpec((ts, d), lambda b, s: (b * nt + s, 0)),
                  pl.BlockSpec((mem_len, d), lambda b, s: (b, 0)),
                  pl.BlockSpec((mem_len, d), lambda b, s: (b, 0))],
        out_specs=pl.BlockSpec((ts, d), lambda b, s: (b * nt + s, 0)),
        out_shape=jax.ShapeDtypeStruct((m, d), BF16),
        compiler_params=_params("arbitrary", "arbitrary"),
        name="cross_attn",
    )(q, k, v)


def _swiglu_tile(h_ref, wgb_ref, wub_ref, o_ref):
    h = h_ref[...]
    g = jnp.dot(h, wgb_ref[...], preferred_element_type=F32)
    u = jnp.dot(h, wub_ref[...], preferred_element_type=F32)
    o_ref[...] = (g * _sigmoid(g) * u).astype(o_ref.dtype)


def _gateup_kernel(h_ref, wg_ref, wu_ref, o_ref, wgb_ref, wub_ref):
    @pl.when(pl.program_id(1) == 0)
    def _():
        wgb_ref[...] = wg_ref[...].astype(BF16)
        wub_ref[...] = wu_ref[...].astype(BF16)

    _swiglu_tile(h_ref, wgb_ref, wub_ref, o_ref)


def _gateup(h, wg, wu, w_lead, tm=1024, tc=512):
    m, d = h.shape
    f = wg.shape[-1]
    tm = min(tm, m)
    nlead = len(w_lead)
    wspec = pl.BlockSpec((None,) * nlead + (d, tc), lambda c, i: tuple(w_lead) + (0, c))
    return pl.pallas_call(
        _gateup_kernel,
        grid=(f // tc, m // tm),
        in_specs=[pl.BlockSpec((tm, d), lambda c, i: (i, 0)), wspec, wspec],
        out_specs=pl.BlockSpec((tm, tc), lambda c, i: (i, c)),
        out_shape=jax.ShapeDtypeStruct((m, f), BF16),
        scratch_shapes=[pltpu.VMEM((d, tc), BF16), pltpu.VMEM((d, tc), BF16)],
        compiler_params=_params("arbitrary", "arbitrary"),
        name="swiglu_gateup",
    )(h, wg, wu)


def _expert_rows_loop(start_ref, tiles_ref, src_hbm, dst_hbm, in_buf, out_buf, sem_in, sem_out, col0, tn, tile_fn):
    e = pl.program_id(1)
    n = tiles_ref[e]
    row0 = start_ref[e]
    tg = in_buf.shape[1]

    def rows(t):
        return pl.ds(pl.multiple_of(row0 + t * tg, tg), tg)

    def load(t, slot):
        return pltpu.make_async_copy(src_hbm.at[rows(t), :], in_buf.at[slot], sem_in.at[slot])

    def store(t, slot):
        return pltpu.make_async_copy(out_buf.at[slot], dst_hbm.at[rows(t), pl.ds(col0, tn)], sem_out.at[slot])

    @pl.when(jnp.logical_and(n > 0, jnp.logical_and(pl.program_id(0) == 0, e == 0)))
    def _():
        load(0, 0).start(priority=LOAD_PRIORITY)

    def body(t, carry):
        slot = lax.rem(t, 2)
        load(t, slot).wait()

        @pl.when(t + 1 < n)
        def _():
            load(t + 1, 1 - slot).start(priority=LOAD_PRIORITY)

        @pl.when(t >= 2)
        def _():
            store(t - 2, slot).wait()

        out_buf[slot] = tile_fn(in_buf[slot])
        store(t, slot).start()
        return carry

    lax.fori_loop(0, n, body, 0)

    @pl.when(n >= 2)
    def _():
        store(n - 2, lax.rem(n, 2)).wait()

    @pl.when(n >= 1)
    def _():
        store(n - 1, lax.rem(n + 1, 2)).wait()


LOAD_PRIORITY = 1


def _prefetch_first_tile(start_ref, tiles_ref, src_hbm, in_buf, sem_in):
    e = pl.program_id(1)
    n_groups = pl.num_programs(1)
    wraps = e == n_groups - 1
    nxt = jnp.where(wraps, 0, e + 1)
    has_step = jnp.logical_or(jnp.logical_not(wraps), pl.program_id(0) + 1 < pl.num_programs(0))
    loads = jnp.logical_and(nxt < n_groups - 1, tiles_ref[nxt] > 0)
    tg = in_buf.shape[1]

    @pl.when(jnp.logical_and(has_step, loads))
    def _():
        rows = pl.ds(pl.multiple_of(start_ref[nxt], tg), tg)
        pltpu.make_async_copy(src_hbm.at[rows, :], in_buf.at[0], sem_in.at[0]).start(priority=LOAD_PRIORITY)


def _zero_rows_loop(start_ref, tiles_ref, dst_hbm, out_buf, sem_out, col0, tn):
    e = pl.program_id(1)
    row0 = start_ref[e]
    tg = out_buf.shape[1]
    out_buf[0] = jnp.zeros(out_buf.shape[1:], out_buf.dtype)

    def body(t, carry):
        rows = pl.ds(pl.multiple_of(row0 + t * tg, tg), tg)
        cp = pltpu.make_async_copy(out_buf.at[0], dst_hbm.at[rows, pl.ds(col0, tn)], sem_out.at[0])
        cp.start()
        cp.wait()
        return carry

    lax.fori_loop(0, tiles_ref[e], body, 0)


def _moe_gateup_kernel(start_ref, tiles_ref, hs_hbm, wg_ref, wu_ref, a_hbm, wgb_ref, wub_ref,
                       in_buf, out_buf, sem_in, sem_out):
    tc = wgb_ref.shape[1]
    col0 = pl.multiple_of(pl.program_id(0) * tc, tc)
    is_expert = pl.program_id(1) < pl.num_programs(1) - 1

    def swiglu(h):
        g = jnp.dot(h, wgb_ref[...], preferred_element_type=F32)
        u = jnp.dot(h, wub_ref[...], preferred_element_type=F32)
        return (g * _sigmoid(g) * u).astype(BF16)

    @pl.when(is_expert)
    def _():
        wgb_ref[...] = wg_ref[...].astype(BF16)
        wub_ref[...] = wu_ref[...].astype(BF16)
        _expert_rows_loop(start_ref, tiles_ref, hs_hbm, a_hbm, in_buf, out_buf, sem_in, sem_out, col0, tc, swiglu)

    @pl.when(jnp.logical_not(is_expert))
    def _():
        _zero_rows_loop(start_ref, tiles_ref, a_hbm, out_buf, sem_out, col0, tc)

    _prefetch_first_tile(start_ref, tiles_ref, hs_hbm, in_buf, sem_in)


def _moe_gateup(g_start, g_tiles, hs, wg, wu, layer, tc=512):
    r, d = hs.shape
    ne, f = wg.shape[1], wg.shape[-1]
    tg = MOE_TILE
    wspec = pl.BlockSpec((None, None, d, tc), lambda c, e, gs, gt: (layer, jnp.minimum(e, ne - 1), 0, c))
    any_spec = pl.BlockSpec(memory_space=pl.ANY)
    return pl.pallas_call(
        _moe_gateup_kernel,
        grid_spec=pltpu.PrefetchScalarGridSpec(
            num_scalar_prefetch=2,
            grid=(f // tc, ne + 1),
            in_specs=[any_spec, wspec, wspec],
            out_specs=any_spec,
            scratch_shapes=[pltpu.VMEM((d, tc), BF16), pltpu.VMEM((d, tc), BF16),
                            pltpu.VMEM((2, tg, d), BF16), pltpu.VMEM((2, tg, tc), BF16),
                            pltpu.SemaphoreType.DMA((2,)), pltpu.SemaphoreType.DMA((2,))]),
        out_shape=jax.ShapeDtypeStruct((r, f), BF16),
        compiler_params=_params("arbitrary", "arbitrary"),
        name="moe_gateup",
    )(g_start, g_tiles, hs, wg, wu)


def _moe_down_kernel(start_ref, tiles_ref, a_hbm, w_ref, y_hbm, wb_ref, in_buf, out_buf, sem_in, sem_out):
    tn = wb_ref.shape[1]
    col0 = pl.multiple_of(pl.program_id(0) * tn, tn)
    is_expert = pl.program_id(1) < pl.num_programs(1) - 1

    def down(a):
        return jnp.dot(a, wb_ref[...], preferred_element_type=F32)

    @pl.when(is_expert)
    def _():
        wb_ref[...] = w_ref[...].astype(BF16)
        _expert_rows_loop(start_ref, tiles_ref, a_hbm, y_hbm, in_buf, out_buf, sem_in, sem_out, col0, tn, down)

    @pl.when(jnp.logical_not(is_expert))
    def _():
        _zero_rows_loop(start_ref, tiles_ref, y_hbm, out_buf, sem_out, col0, tn)

    _prefetch_first_tile(start_ref, tiles_ref, a_hbm, in_buf, sem_in)


def _moe_down(g_start, g_tiles, a, wd, layer, tn=512):
    r, f = a.shape
    ne, d = wd.shape[1], wd.shape[-1]
    tg = MOE_TILE
    any_spec = pl.BlockSpec(memory_space=pl.ANY)
    return pl.pallas_call(
        _moe_down_kernel,
        grid_spec=pltpu.PrefetchScalarGridSpec(
            num_scalar_prefetch=2,
            grid=(d // tn, ne + 1),
            in_specs=[any_spec,
                      pl.BlockSpec((None, None, f, tn), lambda j, e, gs, gt: (layer, jnp.minimum(e, ne - 1), 0, j))],
            out_specs=any_spec,
            scratch_shapes=[pltpu.VMEM((f, tn), BF16),
                            pltpu.VMEM((2, tg, f), BF16), pltpu.VMEM((2, tg, tn), F32),
                            pltpu.SemaphoreType.DMA((2,)), pltpu.SemaphoreType.DMA((2,))]),
        out_shape=jax.ShapeDtypeStruct((r, d), F32),
        compiler_params=_params("arbitrary", "arbitrary"),
        name="moe_down",
    )(g_start, g_tiles, a, wd)


def _expert_changed(te_ref, i):
    return jnp.logical_or(i == 0, te_ref[i] != te_ref[jnp.maximum(i - 1, 0)])


def _moe_gateup_grid_kernel(te_ref, ta_ref, h_ref, wg_ref, wu_ref, o_ref, wgb_ref, wub_ref):
    i = pl.program_id(1)

    @pl.when(_expert_changed(te_ref, i))
    def _():
        wgb_ref[...] = wg_ref[...].astype(BF16)
        wub_ref[...] = wu_ref[...].astype(BF16)

    @pl.when(ta_ref[i] > 0)
    def _():
        _swiglu_tile(h_ref, wgb_ref, wub_ref, o_ref)

    @pl.when(ta_ref[i] == 0)
    def _():
        o_ref[...] = jnp.zeros(o_ref.shape, o_ref.dtype)


def _moe_gateup_grid(tile_e, tile_on, hs, wg, wu, layer, tc=512):
    r, d = hs.shape
    f = wg.shape[-1]
    tg = MOE_TILE
    wspec = pl.BlockSpec((None, None, d, tc), lambda c, i, te, ta: (layer, te[i], 0, c))
    return pl.pallas_call(
        _moe_gateup_grid_kernel,
        grid_spec=pltpu.PrefetchScalarGridSpec(
            num_scalar_prefetch=2,
            grid=(f // tc, r // tg),
            in_specs=[pl.BlockSpec((tg, d), lambda c, i, te, ta: (i, 0)), wspec, wspec],
            out_specs=pl.BlockSpec((tg, tc), lambda c, i, te, ta: (i, c)),
            scratch_shapes=[pltpu.VMEM((d, tc), BF16), pltpu.VMEM((d, tc), BF16)]),
        out_shape=jax.ShapeDtypeStruct((r, f), BF16),
        compiler_params=_params("arbitrary", "arbitrary"),
        name="moe_gateup",
    )(tile_e, tile_on, hs, wg, wu)


def _moe_down_grid_kernel(te_ref, ta_ref, a_ref, w_ref, o_ref, wb_ref):
    i = pl.program_id(1)

    @pl.when(_expert_changed(te_ref, i))
    def _():
        wb_ref[...] = w_ref[...].astype(BF16)

    @pl.when(ta_ref[i] > 0)
    def _():
        o_ref[...] = jnp.dot(a_ref[...], wb_ref[...], preferred_element_type=F32)

    @pl.when(ta_ref[i] == 0)
    def _():
        o_ref[...] = jnp.zeros(o_ref.shape, o_ref.dtype)


def _moe_down_grid(tile_e, tile_on, a, wd, layer, tn=512):
    r, f = a.shape
    d = wd.shape[-1]
    tg = MOE_TILE
    return pl.pallas_call(
        _moe_down_grid_kernel,
        grid_spec=pltpu.PrefetchScalarGridSpec(
            num_scalar_prefetch=2,
            grid=(d // tn, r // tg),
            in_specs=[pl.BlockSpec((tg, f), lambda j, i, te, ta: (i, 0)),
                      pl.BlockSpec((None, None, f, tn), lambda j, i, te, ta: (layer, te[i], 0, j))],
            out_specs=pl.BlockSpec((tg, tn), lambda j, i, te, ta: (i, j)),
            scratch_shapes=[pltpu.VMEM((f, tn), BF16)]),
        out_shape=jax.ShapeDtypeStruct((r, d), F32),
        compiler_params=_params("arbitrary", "arbitrary"),
        name="moe_down",
    )(tile_e, tile_on, a, wd)


def _router_kernel(h_ref, wr_ref, br_ref, o_ref, *, ne):
    logits = jnp.dot(h_ref[...], wr_ref[...].astype(BF16), preferred_element_type=F32) + br_ref[...]
    lane = lax.broadcasted_iota(jnp.int32, logits.shape, 1).astype(F32)
    lg = jnp.where(lane < ne, logits, -jnp.inf)
    v1 = jnp.max(lg, axis=1, keepdims=True)
    i1 = jnp.min(jnp.where(lg == v1, lane, float(LANES)), axis=1, keepdims=True)
    lg2 = jnp.where(lane == i1, -jnp.inf, lg)
    v2 = jnp.max(lg2, axis=1, keepdims=True)
    i2 = jnp.min(jnp.where(lg2 == v2, lane, float(LANES)), axis=1, keepdims=True)
    e2 = jnp.exp(v2 - v1)
    w1 = 1.0 / (1.0 + e2)
    w2 = e2 / (1.0 + e2)
    o_ref[...] = jnp.where(lane == 0, i1, jnp.where(lane == 1, i2,
                           jnp.where(lane == 2, w1, jnp.where(lane == 3, w2, 0.0))))


def _router(h, w_r, b_r, tm=1024):
    m, d = h.shape
    ne = w_r.shape[1]
    tm = min(tm, m)
    wr = jnp.zeros((d, LANES), F32).at[:, :ne].set(w_r)
    br = jnp.zeros((1, LANES), F32).at[0, :ne].set(b_r)
    return pl.pallas_call(
        functools.partial(_router_kernel, ne=ne),
        grid=(m // tm,),
        in_specs=[pl.BlockSpec((tm, d), lambda i: (i, 0)),
                  pl.BlockSpec((d, LANES), lambda i: (0, 0)),
                  pl.BlockSpec((1, LANES), lambda i: (0, 0))],
        out_specs=pl.BlockSpec((tm, LANES), lambda i: (i, 0)),
        out_shape=jax.ShapeDtypeStruct((m, LANES), F32),
        compiler_params=_params("arbitrary"),
        name="moe_router",
    )(h, wr, br)


def _row_copy(src_hbm, row, buf, slot, sem):
    return pltpu.make_async_copy(src_hbm.at[pl.ds(row, 1), :], buf.at[pl.ds(slot, 1), :], sem)


def _dispatch_kernel(src_ref, nrows_ref, x_hbm, g_ref, o_ref, buf, sem, *, rows):
    base = pl.program_id(0) * rows
    active = base < nrows_ref[0]

    @pl.when(active)
    def _():
        def start(c, carry):
            for u in range(DMA_UNROLL):
                r = u * (rows // DMA_UNROLL) + c
                _row_copy(x_hbm, src_ref[base + r], buf, r, sem).start(priority=u % 2)
            return carry

        def wait(c, carry):
            for u in range(DMA_UNROLL):
                _row_copy(x_hbm, 0, buf, c * DMA_UNROLL + u, sem).wait()
            return carry

        lax.fori_loop(0, rows // DMA_UNROLL, start, 0)
        lax.fori_loop(0, rows // DMA_UNROLL, wait, 0)
        x = buf[...]
        ms = jnp.mean(x * x, axis=-1, keepdims=True)
        o_ref[...] = (x * lax.rsqrt(ms + EPS) * g_ref[...]).astype(o_ref.dtype)

    @pl.when(jnp.logical_not(active))
    def _():
        o_ref[...] = jnp.zeros(o_ref.shape, o_ref.dtype)


def _dispatch(src, n_used, x, g, n_rows):
    m, d = x.shape
    rows = GATHER_ROWS
    return pl.pallas_call(
        functools.partial(_dispatch_kernel, rows=rows),
        grid_spec=pltpu.PrefetchScalarGridSpec(
            num_scalar_prefetch=2,
            grid=(n_rows // rows,),
            in_specs=[pl.BlockSpec(memory_space=pl.ANY),
                      pl.BlockSpec((1, d), lambda i, s, n: (0, 0))],
            out_specs=pl.BlockSpec((rows, d), lambda i, s, n: (i, 0)),
            scratch_shapes=[pltpu.VMEM((rows, d), F32), pltpu.SemaphoreType.DMA(())]),
        out_shape=jax.ShapeDtypeStruct((n_rows, d), BF16),
        compiler_params=_params("arbitrary"),
        name="moe_dispatch",
    )(src, n_used, x, g.reshape(1, d))


def _combine_kernel(p1_ref, p2_ref, x_ref, route_ref, g_ref, y_hbm, *rest, rows, emit_x):
    if emit_x:
        o_ref, hn_ref, buf1, buf2, sem = rest
    else:
        hn_ref, buf1, buf2, sem = rest
    base = pl.program_id(0) * rows

    def start(c, carry):
        for u in range(DMA_UNROLL):
            r = c * DMA_UNROLL + u
            _row_copy(y_hbm, p1_ref[base + r], buf1, r, sem.at[0]).start(priority=0)
            _row_copy(y_hbm, p2_ref[base + r], buf2, r, sem.at[1]).start(priority=1)
        return carry

    def wait(c, carry):
        for u in range(DMA_UNROLL):
            r = c * DMA_UNROLL + u
            _row_copy(y_hbm, 0, buf1, r, sem.at[0]).wait()
            _row_copy(y_hbm, 0, buf2, r, sem.at[1]).wait()
        return carry

    lax.fori_loop(0, rows // DMA_UNROLL, start, 0)
    lax.fori_loop(0, rows // DMA_UNROLL, wait, 0)
    route = route_ref[...]
    out = x_ref[...] + route[:, 2:3] * buf1[...] + route[:, 3:4] * buf2[...]
    if emit_x:
        o_ref[...] = out
    ms = jnp.mean(out * out, axis=-1, keepdims=True)
    hn_ref[...] = (out * lax.rsqrt(ms + EPS) * g_ref[...]).astype(hn_ref.dtype)


def _combine(p1, p2, x, route, y, next_gain, next_dtype, emit_x):
    m, d = x.shape
    rows = GATHER_ROWS
    tile = pl.BlockSpec((rows, d), lambda i, a, b: (i, 0))
    out_specs = [tile, tile] if emit_x else [tile]
    out_shape = [jax.ShapeDtypeStruct((m, d), next_dtype)]
    if emit_x:
        out_shape.insert(0, jax.ShapeDtypeStruct((m, d), F32))
    res = pl.pallas_call(
        functools.partial(_combine_kernel, rows=rows, emit_x=emit_x),
        grid_spec=pltpu.PrefetchScalarGridSpec(
            num_scalar_prefetch=2,
            grid=(m // rows,),
            in_specs=[tile,
                      pl.BlockSpec((rows, LANES), lambda i, a, b: (i, 0)),
                      pl.BlockSpec((1, d), lambda i, a, b: (0, 0)),
                      pl.BlockSpec(memory_space=pl.ANY)],
            out_specs=out_specs,
            scratch_shapes=[pltpu.VMEM((rows, d), F32), pltpu.VMEM((rows, d), F32),
                            pltpu.SemaphoreType.DMA((2,))]),
        out_shape=out_shape,
        compiler_params=_params("arbitrary"),
        name="moe_combine",
    )(p1, p2, x, route, next_gain.reshape(1, d), y)
    return (res[0], res[1]) if emit_x else (None, res[0])


def _moe_plan(route, m, ne, tg):
    i1 = route[:, 0].astype(jnp.int32)
    i2 = route[:, 1].astype(jnp.int32)
    experts = jnp.arange(ne, dtype=jnp.int32)
    sel = ((i1[:, None] == experts) | (i2[:, None] == experts)).astype(jnp.int32)
    counts = jnp.sum(sel, axis=0)
    padded = ((counts + tg - 1) // tg) * tg
    ends = jnp.cumsum(padded)
    pos = (ends - padded)[None, :] + jnp.cumsum(sel, axis=0) - sel
    p1 = jnp.take_along_axis(pos, i1[:, None], axis=1)[:, 0]
    p2 = jnp.take_along_axis(pos, i2[:, None], axis=1)[:, 0]
    n_tiles = (2 * m) // tg + ne
    tok = jnp.arange(m, dtype=jnp.int32)
    filler = jnp.arange(n_tiles * tg, dtype=jnp.int32) % m
    src = filler.at[jnp.concatenate([p1, p2])].set(jnp.concatenate([tok, tok]))
    g_start = jnp.concatenate([ends - padded, ends[-1:]])
    g_tiles = jnp.concatenate([padded // tg, n_tiles - ends[-1:] // tg])
    tile_start = jnp.arange(n_tiles, dtype=jnp.int32) * tg
    tile_e = jnp.minimum(jnp.sum((tile_start[:, None] >= ends[None, :]).astype(jnp.int32), axis=1), ne - 1)
    tile_on = (tile_start < ends[-1]).astype(jnp.int32)
    return p1, p2, src, ends[-1:], g_start, g_tiles, tile_e, tile_on, n_tiles * tg


def _moe_block(x, h, g_ffn, w_r, b_r, wg, wu, wd, layer, next_gain, next_dtype, emit_x):
    m, d = x.shape
    route = _router(h, w_r, b_r)
    p1, p2, src, n_used, g_start, g_tiles, tile_e, tile_on, n_rows = _moe_plan(route, m, N_EXPERTS, MOE_TILE)
    hs = _dispatch(src, n_used, x, g_ffn, n_rows)
    a = _moe_gateup_grid(tile_e, tile_on, hs, wg, wu, layer)
    y = _moe_down_grid(tile_e, tile_on, a, wd, layer)
    return _combine(p1, p2, x, route, y, next_gain, next_dtype, emit_x)


def kernel(x, mem, norm_mix, w_in, conv_dw_w, conv_dw_b, conv_ln_g, conv_ln_b, qk_conv_w, qk_conv_b, b_igate, b_fgate, mlstm_norm_g, w_out, norm_cross, norm_mem, w_cq, w_ck, w_cv, w_co, norm_ffn, w_gate_dense, w_up_dense, w_down_dense, w_router, b_router, w_gate_moe, w_up_moe, w_down_moe, norm_final):
    batch, seq, d = x.shape
    mem_len = mem.shape[1]
    depth = norm_mix.shape[0]
    d_conv = conv_dw_w.shape[2]
    d_mlstm = mlstm_norm_g.shape[1]
    n_main = 2 * d_conv + 4 * d_mlstm
    nh = N_MLSTM_HEADS
    xf = x.reshape(batch * seq, d)
    memf = mem.reshape(batch * mem_len, d)

    w_in_nk = jnp.swapaxes(w_in, 1, 2)
    h = _rmsnorm(xf, norm_mix[0], BF16)
    for l in range(depth):
        last = l == depth - 1
        next_gain = norm_final if last else norm_mix[l + 1]
        next_dtype = F32 if last else BF16
        z = _matmul([h], w_in_nk, (l,), n_cols=n_main, w_is_nk=True, tn=1024, name="w_in")
        zgt = _gates(h, w_in_nk, l, n_main, 2 * nh)
        yc = _conformer(z, conv_dw_w[l], conv_dw_b[l], conv_ln_g[l], conv_ln_b[l], batch, seq)
        ym = _mlstm(z, zgt.T, zgt, qk_conv_w[l], qk_conv_b[l], b_igate[l], b_fgate[l], mlstm_norm_g[l],
                    batch, seq)
        xf, h = _matmul([yc, ym], w_out, (l,), residual=xf, norm_gain=norm_cross[l], tm=512, tn=d,
                        name="w_out")
        mem_n = _rmsnorm(memf, norm_mem[l], BF16)
        q = _matmul([h], w_cq, (l,), out_dtype=BF16, tn=1024, name="w_cq")
        kk = _matmul([mem_n], w_ck, (l,), out_dtype=BF16, name="w_ck")
        vv = _matmul([mem_n], w_cv, (l,), out_dtype=BF16, name="w_cv")
        att = _xattn(q, kk, vv, batch, seq, mem_len)
        xf, h = _matmul([att], w_co, (l,), residual=xf, norm_gain=norm_ffn[l], tm=512, tn=d, name="w_co")
        j = l // 2
        if l % 2 == 0:
            a = _gateup(h, w_gate_dense, w_up_dense, (j,))
            xf = _matmul([a], w_down_dense, (j,), residual=xf, tm=512, tn=512, name="w_down")
            h = _rmsnorm(xf, next_gain, next_dtype)
        else:
            xf, h = _moe_block(xf, h, norm_ffn[l], w_router[j], b_router[j], w_gate_moe, w_up_moe,
                               w_down_moe, j, next_gain, next_dtype, emit_x=not last)
    return h.reshape(batch, seq, d)
```

```python
import functools

import jax
import jax.numpy as jnp
from jax import lax
from jax.experimental import pallas as pl
from jax.experimental.pallas import tpu as pltpu

F32 = jnp.float32
BF16 = jnp.bfloat16
EPS = 1e-6

V7X_VMEM_BYTES = 64 * 1024 * 1024
VMEM_LIMIT = V7X_VMEM_BYTES - 8 * 1024 * 1024
LANES = 128
SUBLANES = 8

N_MLSTM_HEADS = 4
N_XHEADS = 4
CONV_WIDTH = 31
QK_CONV_WIDTH = 4
CHUNK = 128
N_EXPERTS = 8
MOE_TILE = 512
GATHER_ROWS = 512
DMA_UNROLL = 8


def _params(*sem):
    return pltpu.CompilerParams(dimension_semantics=sem, vmem_limit_bytes=VMEM_LIMIT)


def _sigmoid(x):
    return 1.0 / (1.0 + jnp.exp(-x))


def _log_sigmoid(x):
    return jnp.minimum(x, 0.0) - jnp.log(1.0 + jnp.exp(-jnp.abs(x)))


def _rmsnorm_kernel(x_ref, g_ref, o_ref):
    x = x_ref[...]
    ms = jnp.mean(x * x, axis=-1, keepdims=True)
    o_ref[...] = (x * lax.rsqrt(ms + EPS) * g_ref[...]).astype(o_ref.dtype)


def _rmsnorm(x, g, out_dtype, tm=512):
    m, d = x.shape
    tm = min(tm, m)
    return pl.pallas_call(
        _rmsnorm_kernel,
        grid=(m // tm,),
        in_specs=[pl.BlockSpec((tm, d), lambda i: (i, 0)),
                  pl.BlockSpec((1, d), lambda i: (0, 0))],
        out_specs=pl.BlockSpec((tm, d), lambda i: (i, 0)),
        out_shape=jax.ShapeDtypeStruct((m, d), out_dtype),
        compiler_params=_params("arbitrary"),
        name="rmsnorm",
    )(x, g.reshape(1, d))


NT_DIMS = (((1,), (1,)), ((), ()))


def _matmul_kernel(*refs, k_sizes, w_is_nk, has_res, has_norm):
    na = len(k_sizes)
    a_refs = refs[:na]
    w_ref = refs[na]
    pos = na + 1
    r_ref = g_ref = hn_ref = None
    if has_res:
        r_ref = refs[pos]
        pos += 1
    if has_norm:
        g_ref = refs[pos]
        pos += 1
    o_ref = refs[pos]
    if has_norm:
        hn_ref = refs[pos + 1]
    wb_ref = refs[-1]

    @pl.when(pl.program_id(1) == 0)
    def _():
        wb_ref[...] = w_ref[...].astype(BF16)

    acc = None
    k0 = 0
    for a_ref, ks in zip(a_refs, k_sizes):
        if w_is_nk:
            part = lax.dot_general(a_ref[...], wb_ref[:, k0:k0 + ks], NT_DIMS, preferred_element_type=F32)
        else:
            part = jnp.dot(a_ref[...], wb_ref[k0:k0 + ks, :], preferred_element_type=F32)
        acc = part if acc is None else acc + part
        k0 += ks
    if has_res:
        acc = acc + r_ref[...]
    o_ref[...] = acc.astype(o_ref.dtype)
    if has_norm:
        ms = jnp.mean(acc * acc, axis=-1, keepdims=True)
        hn_ref[...] = (acc * lax.rsqrt(ms + EPS) * g_ref[...]).astype(hn_ref.dtype)


def _matmul(a_list, w, w_lead=(), *, n_cols=None, w_is_nk=False, residual=None, norm_gain=None,
            out_dtype=F32, tm=1024, tn=512, name="matmul"):
    m = a_list[0].shape[0]
    k_sizes = tuple(a.shape[1] for a in a_list)
    k = sum(k_sizes)
    n_total = w.shape[-2] if w_is_nk else w.shape[-1]
    n = n_total if n_cols is None else n_cols
    tm = min(tm, m)
    tn = min(tn, n)
    nlead = len(w_lead)
    lead = tuple(w_lead)
    if w_is_nk:
        w_block, w_map, wb_shape = (None,) * nlead + (tn, k), (lambda j, i: lead + (j, 0)), (tn, k)
    else:
        w_block, w_map, wb_shape = (None,) * nlead + (k, tn), (lambda j, i: lead + (0, j)), (k, tn)
    w_mode = dict(pipeline_mode=pl.Buffered(1)) if n == tn else {}
    in_specs = [pl.BlockSpec((tm, ks), lambda j, i: (i, 0)) for ks in k_sizes]
    in_specs.append(pl.BlockSpec(w_block, w_map, **w_mode))
    args = list(a_list) + [w]
    out_tile = pl.BlockSpec((tm, tn), lambda j, i: (i, j))
    if residual is not None:
        in_specs.append(out_tile)
        args.append(residual)
    out_specs, out_shape = out_tile, jax.ShapeDtypeStruct((m, n), out_dtype)
    if norm_gain is not None:
        assert tn == n, "the fused RMSNorm needs whole rows"
        in_specs.append(pl.BlockSpec((1, n), lambda j, i: (0, 0)))
        args.append(norm_gain.reshape(1, n))
        out_specs, out_shape = [out_tile, out_tile], [out_shape, jax.ShapeDtypeStruct((m, n), BF16)]
    return pl.pallas_call(
        functools.partial(_matmul_kernel, k_sizes=k_sizes, w_is_nk=w_is_nk,
                          has_res=residual is not None, has_norm=norm_gain is not None),
        grid=(n // tn, m // tm),
        in_specs=in_specs,
        out_specs=out_specs,
        out_shape=out_shape,
        scratch_shapes=[pltpu.VMEM(wb_shape, BF16)],
        compiler_params=_params("arbitrary", "arbitrary"),
        name=name,
    )(*args)


def _gates_kernel(h_ref, w_ref, o_ref):
    o_ref[...] = lax.dot_general(w_ref[...].astype(BF16), h_ref[...], NT_DIMS, preferred_element_type=F32)


def _gates(h, w_nk, lead, row0, nrows, tm=1024):
    m, k = h.shape
    tm = min(tm, m)
    assert row0 % nrows == 0
    return pl.pallas_call(
        _gates_kernel,
        grid=(m // tm,),
        in_specs=[pl.BlockSpec((tm, k), lambda i: (i, 0)),
                  pl.BlockSpec((None, nrows, k), lambda i: (lead, row0 // nrows, 0))],
        out_specs=pl.BlockSpec((nrows, tm), lambda i: (0, i)),
        out_shape=jax.ShapeDtypeStruct((nrows, m), F32),
        compiler_params=_params("arbitrary"),
        name="w_in_gates",
    )(h, w_nk)


CONV_HALO = 32
CONV_BLOCK = 64
CONV_FIRST = CONV_HALO - (CONV_WIDTH - 1)
CONV_LANES = 512


def _conformer_kernel(a_ref, g_ref, w_ref, b_ref, lng_ref, lnb_ref, o_ref, ubuf, sbuf, cbuf, wrep, *, ts):
    s = pl.program_id(1)
    c = a_ref.shape[1]

    @pl.when(jnp.logical_and(pl.program_id(0) == 0, s == 0))
    def _():
        for k in range(CONV_WIDTH):
            wrep[k * SUBLANES:(k + 1) * SUBLANES, :] = jnp.broadcast_to(w_ref[k:k + 1, :], (SUBLANES, c))

    @pl.when(s == 0)
    def _():
        ubuf[0:CONV_HALO, :] = jnp.zeros((CONV_HALO, c), F32)

    @pl.when(s > 0)
    def _():
        ubuf[0:CONV_HALO, :] = ubuf[ts:ts + CONV_HALO, :]

    ubuf[CONV_HALO:CONV_HALO + ts, :] = a_ref[...] * _sigmoid(g_ref[...])
    for sft in range(1, SUBLANES):
        sbuf[sft - 1] = ubuf[sft:sft + sbuf.shape[1], :]
    lng = lng_ref[...]
    lnb = lnb_ref[...]
    nsub = CONV_BLOCK // SUBLANES
    def rows8(start):
        return pl.ds(pl.multiple_of(start, SUBLANES), SUBLANES)

    def row_block(r, carry):
        base = r * CONV_BLOCK
        for lc in range(c // CONV_LANES):
            lanes = slice(lc * CONV_LANES, (lc + 1) * CONV_LANES)
            accs = [jnp.broadcast_to(b_ref[:, lanes], (SUBLANES, CONV_LANES))] * nsub
            for k in range(CONV_WIDTH):
                sft = (CONV_FIRST + k) % SUBLANES
                row = base + (CONV_FIRST + k) - sft
                wk = wrep[k * SUBLANES:(k + 1) * SUBLANES, lanes]
                for i in range(nsub):
                    lo = rows8(row + i * SUBLANES)
                    tap = ubuf[lo, lanes] if sft == 0 else sbuf[sft - 1, lo, lanes]
                    accs[i] = accs[i] + wk * tap
            for i in range(nsub):
                cbuf[rows8(base + i * SUBLANES), lanes] = accs[i]
        blk = pl.ds(pl.multiple_of(base, CONV_BLOCK), CONV_BLOCK)
        y = cbuf[blk, :]
        mu = jnp.mean(y, axis=-1, keepdims=True)
        d = y - mu
        var = jnp.mean(d * d, axis=-1, keepdims=True)
        y = d * lax.rsqrt(var + EPS) * lng + lnb
        o_ref[blk, :] = (y * _sigmoid(y)).astype(o_ref.dtype)
        return carry

    lax.fori_loop(0, ts // CONV_BLOCK, row_block, 0)


def _conformer(z, w_dw, b_dw, ln_g, ln_b, batch, seq, ts=256):
    m = z.shape[0]
    c = w_dw.shape[1]
    ts = min(ts, seq)
    nt = seq // ts
    row = lambda b, s: (b * nt + s, 0)
    vec = pl.BlockSpec((1, c), lambda b, s: (0, 0))
    return pl.pallas_call(
        functools.partial(_conformer_kernel, ts=ts),
        grid=(batch, nt),
        in_specs=[pl.BlockSpec((ts, c), row),
                  pl.BlockSpec((ts, c), lambda b, s: (b * nt + s, 1)),
                  pl.BlockSpec((CONV_WIDTH, c), lambda b, s: (0, 0)),
                  vec, vec, vec],
        out_specs=pl.BlockSpec((ts, c), row),
        out_shape=jax.ShapeDtypeStruct((m, c), BF16),
        scratch_shapes=[pltpu.VMEM((CONV_HALO + ts, c), F32),
                        pltpu.VMEM((SUBLANES - 1, ts + CONV_HALO - SUBLANES, c), F32),
                        pltpu.VMEM((ts, c), F32),
                        pltpu.VMEM((CONV_WIDTH * SUBLANES, c), F32)],
        compiler_params=_params("arbitrary", "arbitrary"),
        name="conformer_conv",
    )(z, z, w_dw, b_dw.reshape(1, c), ln_g.reshape(1, c), ln_b.reshape(1, c))


QK_HALO = 8


def _mlstm_kernel(q_ref, k_ref, v_ref, o_ref, zg_ref, zgt_ref, cwq_ref, cwk_ref, cbq_ref, cbk_ref,
                  bcol_ref, brow_ref, ng_ref, out_ref, qbuf, kbuf, sq, sk, qc_scr, kc_scr, c_scr, n_scr, m_scr,
                  *, nh, dh):
    ci = pl.program_id(1)
    L = q_ref.shape[0]
    w = q_ref.shape[1]

    @pl.when(ci == 0)
    def _():
        qbuf[0:QK_HALO, :] = jnp.zeros((QK_HALO, w), F32)
        kbuf[0:QK_HALO, :] = jnp.zeros((QK_HALO, w), F32)
        c_scr[...] = jnp.zeros(c_scr.shape, F32)
        n_scr[...] = jnp.zeros(n_scr.shape, F32)
        m_scr[...] = jnp.zeros(m_scr.shape, F32)

    @pl.when(ci > 0)
    def _():
        qbuf[0:QK_HALO, :] = qbuf[L:L + QK_HALO, :]
        kbuf[0:QK_HALO, :] = kbuf[L:L + QK_HALO, :]

    qbuf[QK_HALO:QK_HALO + L, :] = q_ref[...]
    kbuf[QK_HALO:QK_HALO + L, :] = k_ref[...]

    def short_conv(buf, sbuf, cw_ref, cb_ref, dst, post_scale):
        last = QK_CONV_WIDTH - 1
        for j in range(last):
            off = QK_HALO - last + j
            sbuf[j] = buf[off:off + L, :]
        for lc in range(w // LANES):
            lanes = slice(lc * LANES, (lc + 1) * LANES)
            acc = cb_ref[:, lanes] + cw_ref[last:last + 1, lanes] * buf[QK_HALO:QK_HALO + L, lanes]
            for j in range(last):
                acc = acc + cw_ref[j:j + 1, lanes] * sbuf[j, :, lanes]
            dst[:, lanes] = acc * _sigmoid(acc) * post_scale

    short_conv(qbuf, sq, cwq_ref, cbq_ref, qc_scr, 1.0)
    short_conv(kbuf, sk, cwk_ref, cbk_ref, kc_scr, dh ** -0.5)

    zg = zg_ref[...] + bcol_ref[...]
    zgt = zgt_ref[...] + brow_ref[...]
    lf_c = _log_sigmoid(zg)
    lf_r = _log_sigmoid(zgt)
    row_i = lax.broadcasted_iota(jnp.int32, (L, L), 0)
    col_i = lax.broadcasted_iota(jnp.int32, (L, L), 1)
    causal = row_i >= col_i
    nt_dims = (((1,), (1,)), ((), ()))
    tn_dims = (((0,), (0,)), ((), ()))
    heads = range(nh)
    cols = [slice(h * dh, (h + 1) * dh) for h in heads]

    li_col = [zg[:, h:h + 1] for h in heads]
    li_row = [zgt[h:h + 1, :] for h in heads]
    lf_row = [lf_r[nh + h:nh + h + 1, :] for h in heads]
    b_col = [jnp.sum(jnp.where(causal, lf_row[h], 0.0), axis=1, keepdims=True) for h in heads]
    b_row = [jnp.sum(jnp.where(row_i <= col_i, lf_c[:, nh + h:nh + h + 1], 0.0), axis=0, keepdims=True)
             for h in heads]
    g = [jnp.sum(lf_row[h], axis=1, keepdims=True) for h in heads]
    m_prev = [m_scr[h:h + 1, 0:1] for h in heads]

    qf = [qc_scr[:, cols[h]] for h in heads]
    kf = [kc_scr[:, cols[h]] for h in heads]
    qb = [x.astype(BF16) for x in qf]
    kb = [x.astype(BF16) for x in kf]
    vb = [v_ref[:, cols[h]].astype(BF16) for h in heads]
    qk = [lax.dot_general(qb[h], kb[h], nt_dims, preferred_element_type=F32) for h in heads]
    c_prev = [c_scr[h] for h in heads]
    n_prev = [n_scr[h:h + 1, :] for h in heads]
    qc = [jnp.dot(qb[h], c_prev[h].astype(BF16), preferred_element_type=F32) for h in heads]

    a_col = [b_col[h] + m_prev[h] for h in heads]
    logw = [jnp.where(causal, b_col[h] - b_row[h] + li_row[h], -jnp.inf) for h in heads]
    m_q = [jnp.maximum(a_col[h], jnp.max(logw[h], axis=1, keepdims=True)) for h in heads]
    s = [qk[h] * jnp.exp(logw[h] - m_q[h]) for h in heads]
    inter = [jnp.exp(a_col[h] - m_q[h]) for h in heads]
    sv = [jnp.dot(s[h].astype(BF16), vb[h], preferred_element_type=F32) for h in heads]

    logu = [g[h] - b_col[h] + li_col[h] for h in heads]
    m_new = [jnp.maximum(g[h] + m_prev[h], jnp.max(logu[h], axis=0, keepdims=True)) for h in heads]
    decay = [jnp.exp(g[h] + m_prev[h] - m_new[h]) for h in heads]
    ku = [kf[h] * jnp.exp(logu[h] - m_new[h]) for h in heads]
    kv = [lax.dot_general(ku[h].astype(BF16), vb[h], tn_dims, preferred_element_type=F32) for h in heads]
    for h in heads:
        c_scr[h] = decay[h] * c_prev[h] + kv[h]
        n_scr[h:h + 1, :] = decay[h] * n_prev[h] + jnp.sum(ku[h], axis=0, keepdims=True)
        m_scr[h:h + 1, :] = jnp.broadcast_to(m_new[h], (1, m_scr.shape[1]))

    den = [inter[h] * jnp.sum(qf[h] * n_prev[h], axis=1, keepdims=True) + jnp.sum(s[h], axis=1, keepdims=True)
           for h in heads]
    rden = [1.0 / jnp.maximum(jnp.abs(den[h]), jnp.exp(-m_q[h])) for h in heads]
    ht = [_sigmoid(o_ref[:, cols[h]]) * ((inter[h] * qc[h] + sv[h]) * rden[h]) for h in heads]
    mu = [jnp.mean(ht[h], axis=-1, keepdims=True) for h in heads]
    dv = [ht[h] - mu[h] for h in heads]
    var = [jnp.mean(dv[h] * dv[h], axis=-1, keepdims=True) for h in heads]
    for h in heads:
        out_ref[:, cols[h]] = (dv[h] * lax.rsqrt(var[h] + EPS) * ng_ref[:, cols[h]]).astype(out_ref.dtype)


def _mlstm(z, zg, zgt, qk_w, qk_b, b_i, b_f, norm_g, batch, seq):
    m = z.shape[0]
    nh = N_MLSTM_HEADS
    w = norm_g.shape[0]
    dh = w // nh
    L = CHUNK
    nc = seq // L
    blk = lambda col: pl.BlockSpec((L, w), lambda b, c: (b * nc + c, col))
    full = lambda shape: pl.BlockSpec(shape, lambda b, c: (0,) * len(shape))
    bias = jnp.concatenate([b_i, b_f]).astype(F32)
    bcol = bias.reshape(1, 2 * nh)
    brow = jnp.broadcast_to(bias[:, None], (2 * nh, L))
    return pl.pallas_call(
        functools.partial(_mlstm_kernel, nh=nh, dh=dh),
        grid=(batch, nc),
        in_specs=[blk(2), blk(3), blk(4), blk(5),
                  pl.BlockSpec((L, 2 * nh), lambda b, c: (b * nc + c, 0)),
                  pl.BlockSpec((2 * nh, L), lambda b, c: (0, b * nc + c)),
                  full((QK_CONV_WIDTH, w)), full((QK_CONV_WIDTH, w)),
                  full((1, w)), full((1, w)),
                  full((1, 2 * nh)), full((2 * nh, L)), full((1, w))],
        out_specs=pl.BlockSpec((L, w), lambda b, c: (b * nc + c, 0)),
        out_shape=jax.ShapeDtypeStruct((m, w), BF16),
        scratch_shapes=[pltpu.VMEM((QK_HALO + L, w), F32),
                        pltpu.VMEM((QK_HALO + L, w), F32),
                        pltpu.VMEM((QK_CONV_WIDTH - 1, L, w), F32),
                        pltpu.VMEM((QK_CONV_WIDTH - 1, L, w), F32),
                        pltpu.VMEM((L, w), F32),
                        pltpu.VMEM((L, w), F32),
                        pltpu.VMEM((nh, dh, dh), F32),
                        pltpu.VMEM((SUBLANES, dh), F32),
                        pltpu.VMEM((SUBLANES, LANES), F32)],
        compiler_params=_params("arbitrary", "arbitrary"),
        name="mlstm",
    )(z, z, z, z, zg, zgt, qk_w[:, :w], qk_w[:, w:], qk_b[:w].reshape(1, w), qk_b[w:].reshape(1, w),
      bcol, brow, norm_g.reshape(1, w))


def _xattn_kernel(q_ref, k_ref, v_ref, o_ref, *, nh):
    d = q_ref.shape[1]
    dh = d // nh
    scale = dh ** -0.5
    nt_dims = (((1,), (1,)), ((), ()))
    for h in range(nh):
        lo, hi = h * dh, (h + 1) * dh
        sc = lax.dot_general(q_ref[:, lo:hi], k_ref[:, lo:hi], nt_dims,
                             preferred_element_type=F32) * scale
        mx = jnp.max(sc, axis=-1, keepdims=True)
        e = jnp.exp(sc - mx)
        p = e / jnp.sum(e, axis=-1, keepdims=True)
        o_ref[:, lo:hi] = jnp.dot(p.astype(BF16), v_ref[:, lo:hi],
                                  preferred_element_type=F32).astype(o_ref.dtype)


def _xattn(q, k, v, batch, seq, mem_len, ts=512):
    m, d = q.shape
    ts = min(ts, seq)
    nt = seq // ts
    return pl.pallas_call(
        functools.partial(_xattn_kernel, nh=N_XHEADS),
        grid=(batch, nt),
        in_specs=[pl.BlockSpec((ts, d), lambda b, s: (b * nt + s, 0)),
                  pl.BlockSpec((mem_len, d), lambda b, s: (b, 0)),
                  pl.BlockSpec((mem_len, d), lambda b, s: (b, 0))],
        out_specs=pl.BlockSpec((ts, d), lambda b, s: (b * nt + s, 0)),
        out_shape=jax.ShapeDtypeStruct((m, d), BF16),
        compiler_params=_params("arbitrary", "arbitrary"),
        name="cross_attn",
    )(q, k, v)


def _swiglu_tile(h_ref, wgb_ref, wub_ref, o_ref):
    h = h_ref[...]
    g = jnp.dot(h, wgb_ref[...], preferred_element_type=F32)
    u = jnp.dot(h, wub_ref[...], preferred_element_type=F32)
    o_ref[...] = (g * _sigmoid(g) * u).astype(o_ref.dtype)


def _gateup_kernel(h_ref, wg_ref, wu_ref, o_ref, wgb_ref, wub_ref):
    @pl.when(pl.program_id(1) == 0)
    def _():
        wgb_ref[...] = wg_ref[...].astype(BF16)
        wub_ref[...] = wu_ref[...].astype(BF16)

    _swiglu_tile(h_ref, wgb_ref, wub_ref, o_ref)


def _gateup(h, wg, wu, w_lead, tm=1024, tc=512):
    m, d = h.shape
    f = wg.shape[-1]
    tm = min(tm, m)
    nlead = len(w_lead)
    wspec = pl.BlockSpec((None,) * nlead + (d, tc), lambda c, i: tuple(w_lead) + (0, c))
    return pl.pallas_call(
        _gateup_kernel,
        grid=(f // tc, m // tm),
        in_specs=[pl.BlockSpec((tm, d), lambda c, i: (i, 0)), wspec, wspec],
        out_specs=pl.BlockSpec((tm, tc), lambda c, i: (i, c)),
        out_shape=jax.ShapeDtypeStruct((m, f), BF16),
        scratch_shapes=[pltpu.VMEM((d, tc), BF16), pltpu.VMEM((d, tc), BF16)],
        compiler_params=_params("arbitrary", "arbitrary"),
        name="swiglu_gateup",
    )(h, wg, wu)


def _expert_changed(te_ref, i):
    return jnp.logical_or(i == 0, te_ref[i] != te_ref[jnp.maximum(i - 1, 0)])


def _moe_gateup_kernel(te_ref, ta_ref, h_ref, wg_ref, wu_ref, o_ref):
    i = pl.program_id(1)

    @pl.when(ta_ref[i] > 0)
    def _():
        h = h_ref[...]
        g = jnp.dot(h, wg_ref[...].astype(BF16), preferred_element_type=F32)
        u = jnp.dot(h, wu_ref[...].astype(BF16), preferred_element_type=F32)
        o_ref[...] = (g * _sigmoid(g) * u).astype(o_ref.dtype)

    @pl.when(ta_ref[i] == 0)
    def _():
        o_ref[...] = jnp.zeros(o_ref.shape, o_ref.dtype)


def _moe_gateup(tile_e, tile_on, hs, wg, wu, layer, tc=512):
    r, d = hs.shape
    f = wg.shape[-1]
    tg = MOE_TILE
    wspec = pl.BlockSpec((None, None, d, tc), lambda c, i, te, ta: (layer, te[i], 0, c))
    return pl.pallas_call(
        _moe_gateup_kernel,
        grid_spec=pltpu.PrefetchScalarGridSpec(
            num_scalar_prefetch=2,
            grid=(f // tc, r // tg),
            in_specs=[pl.BlockSpec((tg, d), lambda c, i, te, ta: (i, 0)), wspec, wspec],
            out_specs=pl.BlockSpec((tg, tc), lambda c, i, te, ta: (i, c))),
        out_shape=jax.ShapeDtypeStruct((r, f), BF16),
        compiler_params=_params("arbitrary", "arbitrary"),
        name="moe_gateup",
    )(tile_e, tile_on, hs, wg, wu)


def _moe_down_kernel(te_ref, ta_ref, a_ref, w_ref, o_ref, wb_ref):
    i = pl.program_id(1)

    @pl.when(_expert_changed(te_ref, i))
    def _():
        wb_ref[...] = w_ref[...].astype(BF16)

    @pl.when(ta_ref[i] > 0)
    def _():
        o_ref[...] = jnp.dot(a_ref[...], wb_ref[...], preferred_element_type=F32)

    @pl.when(ta_ref[i] == 0)
    def _():
        o_ref[...] = jnp.zeros(o_ref.shape, o_ref.dtype)


def _moe_down(tile_e, tile_on, a, wd, layer, tn=512):
    r, f = a.shape
    d = wd.shape[-1]
    tg = MOE_TILE
    return pl.pallas_call(
        _moe_down_kernel,
        grid_spec=pltpu.PrefetchScalarGridSpec(
            num_scalar_prefetch=2,
            grid=(d // tn, r // tg),
            in_specs=[pl.BlockSpec((tg, f), lambda j, i, te, ta: (i, 0)),
                      pl.BlockSpec((None, None, f, tn), lambda j, i, te, ta: (layer, te[i], 0, j))],
            out_specs=pl.BlockSpec((tg, tn), lambda j, i, te, ta: (i, j)),
            scratch_shapes=[pltpu.VMEM((f, tn), BF16)]),
        out_shape=jax.ShapeDtypeStruct((r, d), F32),
        compiler_params=_params("arbitrary", "arbitrary"),
        name="moe_down",
    )(tile_e, tile_on, a, wd)


def _router_kernel(h_ref, wr_ref, br_ref, o_ref, *, ne):
    logits = jnp.dot(h_ref[...], wr_ref[...].astype(BF16), preferred_element_type=F32) + br_ref[...]
    lane = lax.broadcasted_iota(jnp.int32, logits.shape, 1).astype(F32)
    lg = jnp.where(lane < ne, logits, -jnp.inf)
    v1 = jnp.max(lg, axis=1, keepdims=True)
    i1 = jnp.min(jnp.where(lg == v1, lane, float(LANES)), axis=1, keepdims=True)
    lg2 = jnp.where(lane == i1, -jnp.inf, lg)
    v2 = jnp.max(lg2, axis=1, keepdims=True)
    i2 = jnp.min(jnp.where(lg2 == v2, lane, float(LANES)), axis=1, keepdims=True)
    e2 = jnp.exp(v2 - v1)
    w1 = 1.0 / (1.0 + e2)
    w2 = e2 / (1.0 + e2)
    o_ref[...] = jnp.where(lane == 0, i1, jnp.where(lane == 1, i2,
                           jnp.where(lane == 2, w1, jnp.where(lane == 3, w2, 0.0))))


def _router(h, w_r, b_r, tm=1024):
    m, d = h.shape
    ne = w_r.shape[1]
    tm = min(tm, m)
    wr = jnp.zeros((d, LANES), F32).at[:, :ne].set(w_r)
    br = jnp.zeros((1, LANES), F32).at[0, :ne].set(b_r)
    return pl.pallas_call(
        functools.partial(_router_kernel, ne=ne),
        grid=(m // tm,),
        in_specs=[pl.BlockSpec((tm, d), lambda i: (i, 0)),
                  pl.BlockSpec((d, LANES), lambda i: (0, 0)),
                  pl.BlockSpec((1, LANES), lambda i: (0, 0))],
        out_specs=pl.BlockSpec((tm, LANES), lambda i: (i, 0)),
        out_shape=jax.ShapeDtypeStruct((m, LANES), F32),
        compiler_params=_params("arbitrary"),
        name="moe_router",
    )(h, wr, br)


def _row_copy(src_hbm, row, buf, slot, sem):
    return pltpu.make_async_copy(src_hbm.at[pl.ds(row, 1), :], buf.at[pl.ds(slot, 1), :], sem)


def _dispatch_kernel(src_ref, nrows_ref, x_hbm, g_ref, o_ref, buf, sem, *, rows):
    base = pl.program_id(0) * rows
    active = base < nrows_ref[0]

    @pl.when(active)
    def _():
        def start(c, carry):
            for u in range(DMA_UNROLL):
                r = u * (rows // DMA_UNROLL) + c
                _row_copy(x_hbm, src_ref[base + r], buf, r, sem).start(priority=u % 2)
            return carry

        def wait(c, carry):
            for u in range(DMA_UNROLL):
                _row_copy(x_hbm, 0, buf, c * DMA_UNROLL + u, sem).wait()
            return carry

        lax.fori_loop(0, rows // DMA_UNROLL, start, 0)
        lax.fori_loop(0, rows // DMA_UNROLL, wait, 0)
        x = buf[...]
        ms = jnp.mean(x * x, axis=-1, keepdims=True)
        o_ref[...] = (x * lax.rsqrt(ms + EPS) * g_ref[...]).astype(o_ref.dtype)

    @pl.when(jnp.logical_not(active))
    def _():
        o_ref[...] = jnp.zeros(o_ref.shape, o_ref.dtype)


def _dispatch(src, n_used, x, g, n_rows):
    m, d = x.shape
    rows = GATHER_ROWS
    return pl.pallas_call(
        functools.partial(_dispatch_kernel, rows=rows),
        grid_spec=pltpu.PrefetchScalarGridSpec(
            num_scalar_prefetch=2,
            grid=(n_rows // rows,),
            in_specs=[pl.BlockSpec(memory_space=pl.ANY),
                      pl.BlockSpec((1, d), lambda i, s, n: (0, 0))],
            out_specs=pl.BlockSpec((rows, d), lambda i, s, n: (i, 0)),
            scratch_shapes=[pltpu.VMEM((rows, d), F32), pltpu.SemaphoreType.DMA(())]),
        out_shape=jax.ShapeDtypeStruct((n_rows, d), BF16),
        compiler_params=_params("arbitrary"),
        name="moe_dispatch",
    )(src, n_used, x, g.reshape(1, d))


def _combine_kernel(p1_ref, p2_ref, x_ref, route_ref, g_ref, y_hbm, *rest, rows, emit_x):
    if emit_x:
        o_ref, hn_ref, buf1, buf2, sem = rest
    else:
        hn_ref, buf1, buf2, sem = rest
    base = pl.program_id(0) * rows

    def start(c, carry):
        for u in range(DMA_UNROLL):
            r = c * DMA_UNROLL + u
            _row_copy(y_hbm, p1_ref[base + r], buf1, r, sem.at[0]).start(priority=0)
            _row_copy(y_hbm, p2_ref[base + r], buf2, r, sem.at[1]).start(priority=1)
        return carry

    def wait(c, carry):
        for u in range(DMA_UNROLL):
            r = c * DMA_UNROLL + u
            _row_copy(y_hbm, 0, buf1, r, sem.at[0]).wait()
            _row_copy(y_hbm, 0, buf2, r, sem.at[1]).wait()
        return carry

    lax.fori_loop(0, rows // DMA_UNROLL, start, 0)
    lax.fori_loop(0, rows // DMA_UNROLL, wait, 0)
    route = route_ref[...]
    out = x_ref[...] + route[:, 2:3] * buf1[...] + route[:, 3:4] * buf2[...]
    if emit_x:
        o_ref[...] = out
    ms = jnp.mean(out * out, axis=-1, keepdims=True)
    hn_ref[...] = (out * lax.rsqrt(ms + EPS) * g_ref[...]).astype(hn_ref.dtype)


def _combine(p1, p2, x, route, y, next_gain, next_dtype, emit_x):
    m, d = x.shape
    rows = GATHER_ROWS
    tile = pl.BlockSpec((rows, d), lambda i, a, b: (i, 0))
    out_specs = [tile, tile] if emit_x else [tile]
    out_shape = [jax.ShapeDtypeStruct((m, d), next_dtype)]
    if emit_x:
        out_shape.insert(0, jax.ShapeDtypeStruct((m, d), F32))
    res = pl.pallas_call(
        functools.partial(_combine_kernel, rows=rows, emit_x=emit_x),
        grid_spec=pltpu.PrefetchScalarGridSpec(
            num_scalar_prefetch=2,
            grid=(m // rows,),
            in_specs=[tile,
                      pl.BlockSpec((rows, LANES), lambda i, a, b: (i, 0)),
                      pl.BlockSpec((1, d), lambda i, a, b: (0, 0)),
                      pl.BlockSpec(memory_space=pl.ANY)],
            out_specs=out_specs,
            scratch_shapes=[pltpu.VMEM((rows, d), F32), pltpu.VMEM((rows, d), F32),
                            pltpu.SemaphoreType.DMA((2,))]),
        out_shape=out_shape,
        compiler_params=_params("arbitrary"),
        name="moe_combine",
    )(p1, p2, x, route, next_gain.reshape(1, d), y)
    return (res[0], res[1]) if emit_x else (None, res[0])


def _moe_plan(route, m, ne, tg):
    i1 = route[:, 0].astype(jnp.int32)
    i2 = route[:, 1].astype(jnp.int32)
    experts = jnp.arange(ne, dtype=jnp.int32)
    sel = ((i1[:, None] == experts) | (i2[:, None] == experts)).astype(jnp.int32)
    counts = jnp.sum(sel, axis=0)
    padded = ((counts + tg - 1) // tg) * tg
    ends = jnp.cumsum(padded)
    pos = (ends - padded)[None, :] + jnp.cumsum(sel, axis=0) - sel
    p1 = jnp.take_along_axis(pos, i1[:, None], axis=1)[:, 0]
    p2 = jnp.take_along_axis(pos, i2[:, None], axis=1)[:, 0]
    n_tiles = (2 * m) // tg + ne
    tok = jnp.arange(m, dtype=jnp.int32)
    filler = jnp.arange(n_tiles * tg, dtype=jnp.int32) % m
    src = filler.at[jnp.concatenate([p1, p2])].set(jnp.concatenate([tok, tok]))
    tile_start = jnp.arange(n_tiles, dtype=jnp.int32) * tg
    tile_e = jnp.minimum(jnp.sum((tile_start[:, None] >= ends[None, :]).astype(jnp.int32), axis=1), ne - 1)
    tile_on = (tile_start < ends[-1]).astype(jnp.int32)
    return p1, p2, src, ends[-1:], tile_e, tile_on, n_tiles * tg


def _moe_block(x, h, g_ffn, w_r, b_r, wg, wu, wd, layer, next_gain, next_dtype, emit_x):
    m, d = x.shape
    route = _router(h, w_r, b_r)
    p1, p2, src, n_used, tile_e, tile_on, n_rows = _moe_plan(route, m, N_EXPERTS, MOE_TILE)
    hs = _dispatch(src, n_used, x, g_ffn, n_rows)
    a = _moe_gateup(tile_e, tile_on, hs, wg, wu, layer)
    y = _moe_down(tile_e, tile_on, a, wd, layer)
    return _combine(p1, p2, x, route, y, next_gain, next_dtype, emit_x)


def kernel(x, mem, norm_mix, w_in, conv_dw_w, conv_dw_b, conv_ln_g, conv_ln_b, qk_conv_w, qk_conv_b, b_igate, b_fgate, mlstm_norm_g, w_out, norm_cross, norm_mem, w_cq, w_ck, w_cv, w_co, norm_ffn, w_gate_dense, w_up_dense, w_down_dense, w_router, b_router, w_gate_moe, w_up_moe, w_down_moe, norm_final):
    batch, seq, d = x.shape
    mem_len = mem.shape[1]
    depth = norm_mix.shape[0]
    d_conv = conv_dw_w.shape[2]
    d_mlstm = mlstm_norm_g.shape[1]
    n_main = 2 * d_conv + 4 * d_mlstm
    nh = N_MLSTM_HEADS
    xf = x.reshape(batch * seq, d)
    memf = mem.reshape(batch * mem_len, d)

    w_in_nk = jnp.swapaxes(w_in, 1, 2)
    h = _rmsnorm(xf, norm_mix[0], BF16)
    for l in range(depth):
        last = l == depth - 1
        next_gain = norm_final if last else norm_mix[l + 1]
        next_dtype = F32 if last else BF16
        z = _matmul([h], w_in_nk, (l,), n_cols=n_main, w_is_nk=True, tn=1024, name="w_in")
        zgt = _gates(h, w_in_nk, l, n_main, 2 * nh)
        yc = _conformer(z, conv_dw_w[l], conv_dw_b[l], conv_ln_g[l], conv_ln_b[l], batch, seq)
        ym = _mlstm(z, zgt.T, zgt, qk_conv_w[l], qk_conv_b[l], b_igate[l], b_fgate[l], mlstm_norm_g[l],
                    batch, seq)
        xf, h = _matmul([yc, ym], w_out, (l,), residual=xf, norm_gain=norm_cross[l], tm=512, tn=d,
                        name="w_out")
        mem_n = _rmsnorm(memf, norm_mem[l], BF16)
        q = _matmul([h], w_cq, (l,), out_dtype=BF16, tn=1024, name="w_cq")
        kk = _matmul([mem_n], w_ck, (l,), out_dtype=BF16, name="w_ck")
        vv = _matmul([mem_n], w_cv, (l,), out_dtype=BF16, name="w_cv")
        att = _xattn(q, kk, vv, batch, seq, mem_len)
        xf, h = _matmul([att], w_co, (l,), residual=xf, norm_gain=norm_ffn[l], tm=512, tn=d, name="w_co")
        j = l // 2
        if l % 2 == 0:
            a = _gateup(h, w_gate_dense, w_up_dense, (j,))
            xf = _matmul([a], w_down_dense, (j,), residual=xf, tm=512, tn=512, name="w_down")
            h = _rmsnorm(xf, next_gain, next_dtype)
        else:
            xf, h = _moe_block(xf, h, norm_ffn[l], w_router[j], b_router[j], w_gate_moe, w_up_moe,
                               w_down_moe, j, next_gain, next_dtype, emit_x=not last)
    return h.reshape(batch, seq, d)
```

```python
import functools

import jax
import jax.numpy as jnp
from jax import lax
from jax.experimental import pallas as pl
from jax.experimental.pallas import tpu as pltpu

F32 = jnp.float32
BF16 = jnp.bfloat16
EPS = 1e-6

V7X_VMEM_BYTES = 64 * 1024 * 1024
VMEM_LIMIT = V7X_VMEM_BYTES - 8 * 1024 * 1024
LANES = 128
SUBLANES = 8

N_MLSTM_HEADS = 4
N_XHEADS = 4
CONV_WIDTH = 31
QK_CONV_WIDTH = 4
CHUNK = 128
N_EXPERTS = 8
MOE_TILE = 512
GATHER_ROWS = 512
DMA_UNROLL = 8


def _params(*sem):
    return pltpu.CompilerParams(dimension_semantics=sem, vmem_limit_bytes=VMEM_LIMIT)


def _sigmoid(x):
    return 1.0 / (1.0 + jnp.exp(-x))


def _log_sigmoid(x):
    return jnp.minimum(x, 0.0) - jnp.log(1.0 + jnp.exp(-jnp.abs(x)))


def _rmsnorm_kernel(x_ref, g_ref, o_ref):
    x = x_ref[...]
    ms = jnp.mean(x * x, axis=-1, keepdims=True)
    o_ref[...] = (x * lax.rsqrt(ms + EPS) * g_ref[...]).astype(o_ref.dtype)


def _rmsnorm(x, g, out_dtype, tm=512):
    m, d = x.shape
    tm = min(tm, m)
    return pl.pallas_call(
        _rmsnorm_kernel,
        grid=(m // tm,),
        in_specs=[pl.BlockSpec((tm, d), lambda i: (i, 0)),
                  pl.BlockSpec((1, d), lambda i: (0, 0))],
        out_specs=pl.BlockSpec((tm, d), lambda i: (i, 0)),
        out_shape=jax.ShapeDtypeStruct((m, d), out_dtype),
        compiler_params=_params("arbitrary"),
        name="rmsnorm",
    )(x, g.reshape(1, d))


NT_DIMS = (((1,), (1,)), ((), ()))


def _matmul_kernel(*refs, k_sizes, w_is_nk, has_res, has_norm):
    na = len(k_sizes)
    a_refs = refs[:na]
    w_ref = refs[na]
    pos = na + 1
    r_ref = g_ref = hn_ref = None
    if has_res:
        r_ref = refs[pos]
        pos += 1
    if has_norm:
        g_ref = refs[pos]
        pos += 1
    o_ref = refs[pos]
    if has_norm:
        hn_ref = refs[pos + 1]
    wb_ref = refs[-1]

    @pl.when(pl.program_id(1) == 0)
    def _():
        wb_ref[...] = w_ref[...].astype(BF16)

    acc = None
    k0 = 0
    for a_ref, ks in zip(a_refs, k_sizes):
        if w_is_nk:
            part = lax.dot_general(a_ref[...], wb_ref[:, k0:k0 + ks], NT_DIMS, preferred_element_type=F32)
        else:
            part = jnp.dot(a_ref[...], wb_ref[k0:k0 + ks, :], preferred_element_type=F32)
        acc = part if acc is None else acc + part
        k0 += ks
    if has_res:
        acc = acc + r_ref[...]
    o_ref[...] = acc.astype(o_ref.dtype)
    if has_norm:
        ms = jnp.mean(acc * acc, axis=-1, keepdims=True)
        hn_ref[...] = (acc * lax.rsqrt(ms + EPS) * g_ref[...]).astype(hn_ref.dtype)


def _matmul(a_list, w, w_lead=(), *, n_cols=None, w_is_nk=False, residual=None, norm_gain=None,
            out_dtype=F32, tm=1024, tn=512, name="matmul"):
    m = a_list[0].shape[0]
    k_sizes = tuple(a.shape[1] for a in a_list)
    k = sum(k_sizes)
    n_total = w.shape[-2] if w_is_nk else w.shape[-1]
    n = n_total if n_cols is None else n_cols
    tm = min(tm, m)
    tn = min(tn, n)
    nlead = len(w_lead)
    lead = tuple(w_lead)
    if w_is_nk:
        w_block, w_map, wb_shape = (None,) * nlead + (tn, k), (lambda j, i: lead + (j, 0)), (tn, k)
    else:
        w_block, w_map, wb_shape = (None,) * nlead + (k, tn), (lambda j, i: lead + (0, j)), (k, tn)
    w_mode = dict(pipeline_mode=pl.Buffered(1)) if n == tn else {}
    in_specs = [pl.BlockSpec((tm, ks), lambda j, i: (i, 0)) for ks in k_sizes]
    in_specs.append(pl.BlockSpec(w_block, w_map, **w_mode))
    args = list(a_list) + [w]
    out_tile = pl.BlockSpec((tm, tn), lambda j, i: (i, j))
    if residual is not None:
        in_specs.append(out_tile)
        args.append(residual)
    out_specs, out_shape = out_tile, jax.ShapeDtypeStruct((m, n), out_dtype)
    if norm_gain is not None:
        assert tn == n, "the fused RMSNorm needs whole rows"
        in_specs.append(pl.BlockSpec((1, n), lambda j, i: (0, 0)))
        args.append(norm_gain.reshape(1, n))
        out_specs, out_shape = [out_tile, out_tile], [out_shape, jax.ShapeDtypeStruct((m, n), BF16)]
    return pl.pallas_call(
        functools.partial(_matmul_kernel, k_sizes=k_sizes, w_is_nk=w_is_nk,
                          has_res=residual is not None, has_norm=norm_gain is not None),
        grid=(n // tn, m // tm),
        in_specs=in_specs,
        out_specs=out_specs,
        out_shape=out_shape,
        scratch_shapes=[pltpu.VMEM(wb_shape, BF16)],
        compiler_params=_params("arbitrary", "arbitrary"),
        name=name,
    )(*args)


def _gates_kernel(h_ref, w_ref, o_ref):
    o_ref[...] = lax.dot_general(w_ref[...].astype(BF16), h_ref[...], NT_DIMS, preferred_element_type=F32)


def _gates(h, w_nk, lead, row0, nrows, tm=1024):
    m, k = h.shape
    tm = min(tm, m)
    assert row0 % nrows == 0
    return pl.pallas_call(
        _gates_kernel,
        grid=(m // tm,),
        in_specs=[pl.BlockSpec((tm, k), lambda i: (i, 0)),
                  pl.BlockSpec((None, nrows, k), lambda i: (lead, row0 // nrows, 0))],
        out_specs=pl.BlockSpec((nrows, tm), lambda i: (0, i)),
        out_shape=jax.ShapeDtypeStruct((nrows, m), F32),
        compiler_params=_params("arbitrary"),
        name="w_in_gates",
    )(h, w_nk)


CONV_HALO = 32
CONV_BLOCK = 64
CONV_FIRST = CONV_HALO - (CONV_WIDTH - 1)
CONV_LANES = 512


def _conformer_kernel(a_ref, g_ref, w_ref, b_ref, lng_ref, lnb_ref, o_ref, ubuf, sbuf, cbuf, wrep, *, ts):
    s = pl.program_id(1)
    c = a_ref.shape[1]

    @pl.when(jnp.logical_and(pl.program_id(0) == 0, s == 0))
    def _():
        for k in range(CONV_WIDTH):
            wrep[k * SUBLANES:(k + 1) * SUBLANES, :] = jnp.broadcast_to(w_ref[k:k + 1, :], (SUBLANES, c))

    @pl.when(s == 0)
    def _():
        ubuf[0:CONV_HALO, :] = jnp.zeros((CONV_HALO, c), F32)

    @pl.when(s > 0)
    def _():
        ubuf[0:CONV_HALO, :] = ubuf[ts:ts + CONV_HALO, :]

    ubuf[CONV_HALO:CONV_HALO + ts, :] = a_ref[...] * _sigmoid(g_ref[...])
    for sft in range(1, SUBLANES):
        sbuf[sft - 1] = ubuf[sft:sft + sbuf.shape[1], :]
    lng = lng_ref[...]
    lnb = lnb_ref[...]
    nsub = CONV_BLOCK // SUBLANES
    def rows8(start):
        return pl.ds(pl.multiple_of(start, SUBLANES), SUBLANES)

    def row_block(r, carry):
        base = r * CONV_BLOCK
        for lc in range(c // CONV_LANES):
            lanes = slice(lc * CONV_LANES, (lc + 1) * CONV_LANES)
            accs = [jnp.broadcast_to(b_ref[:, lanes], (SUBLANES, CONV_LANES))] * nsub
            for k in range(CONV_WIDTH):
                sft = (CONV_FIRST + k) % SUBLANES
                row = base + (CONV_FIRST + k) - sft
                wk = wrep[k * SUBLANES:(k + 1) * SUBLANES, lanes]
                for i in range(nsub):
                    lo = rows8(row + i * SUBLANES)
                    tap = ubuf[lo, lanes] if sft == 0 else sbuf[sft - 1, lo, lanes]
                    accs[i] = accs[i] + wk * tap
            for i in range(nsub):
                cbuf[rows8(base + i * SUBLANES), lanes] = accs[i]
        blk = pl.ds(pl.multiple_of(base, CONV_BLOCK), CONV_BLOCK)
        y = cbuf[blk, :]
        mu = jnp.mean(y, axis=-1, keepdims=True)
        d = y - mu
        var = jnp.mean(d * d, axis=-1, keepdims=True)
        y = d * lax.rsqrt(var + EPS) * lng + lnb
        o_ref[blk, :] = (y * _sigmoid(y)).astype(o_ref.dtype)
        return carry

    lax.fori_loop(0, ts // CONV_BLOCK, row_block, 0)


def _conformer(z, w_dw, b_dw, ln_g, ln_b, batch, seq, ts=256):
    m = z.shape[0]
    c = w_dw.shape[1]
    ts = min(ts, seq)
    nt = seq // ts
    row = lambda b, s: (b * nt + s, 0)
    vec = pl.BlockSpec((1, c), lambda b, s: (0, 0))
    return pl.pallas_call(
        functools.partial(_conformer_kernel, ts=ts),
        grid=(batch, nt),
        in_specs=[pl.BlockSpec((ts, c), row),
                  pl.BlockSpec((ts, c), lambda b, s: (b * nt + s, 1)),
                  pl.BlockSpec((CONV_WIDTH, c), lambda b, s: (0, 0)),
                  vec, vec, vec],
        out_specs=pl.BlockSpec((ts, c), row),
        out_shape=jax.ShapeDtypeStruct((m, c), BF16),
        scratch_shapes=[pltpu.VMEM((CONV_HALO + ts, c), F32),
                        pltpu.VMEM((SUBLANES - 1, ts + CONV_HALO - SUBLANES, c), F32),
                        pltpu.VMEM((ts, c), F32),
                        pltpu.VMEM((CONV_WIDTH * SUBLANES, c), F32)],
        compiler_params=_params("arbitrary", "arbitrary"),
        name="conformer_conv",
    )(z, z, w_dw, b_dw.reshape(1, c), ln_g.reshape(1, c), ln_b.reshape(1, c))


QK_HALO = 8


def _mlstm_kernel(q_ref, k_ref, v_ref, o_ref, zg_ref, zgt_ref, cwq_ref, cwk_ref, cbq_ref, cbk_ref,
                  bcol_ref, brow_ref, ng_ref, out_ref, qbuf, kbuf, sq, sk, qc_scr, kc_scr, c_scr, n_scr, m_scr,
                  *, nh, dh):
    ci = pl.program_id(1)
    L = q_ref.shape[0]
    w = q_ref.shape[1]

    @pl.when(ci == 0)
    def _():
        qbuf[0:QK_HALO, :] = jnp.zeros((QK_HALO, w), F32)
        kbuf[0:QK_HALO, :] = jnp.zeros((QK_HALO, w), F32)
        c_scr[...] = jnp.zeros(c_scr.shape, F32)
        n_scr[...] = jnp.zeros(n_scr.shape, F32)
        m_scr[...] = jnp.zeros(m_scr.shape, F32)

    @pl.when(ci > 0)
    def _():
        qbuf[0:QK_HALO, :] = qbuf[L:L + QK_HALO, :]
        kbuf[0:QK_HALO, :] = kbuf[L:L + QK_HALO, :]

    qbuf[QK_HALO:QK_HALO + L, :] = q_ref[...]
    kbuf[QK_HALO:QK_HALO + L, :] = k_ref[...]

    def short_conv(buf, sbuf, cw_ref, cb_ref, dst, post_scale):
        last = QK_CONV_WIDTH - 1
        for j in range(last):
            off = QK_HALO - last + j
            sbuf[j] = buf[off:off + L, :]
        for lc in range(w // LANES):
            lanes = slice(lc * LANES, (lc + 1) * LANES)
            acc = cb_ref[:, lanes] + cw_ref[last:last + 1, lanes] * buf[QK_HALO:QK_HALO + L, lanes]
            for j in range(last):
                acc = acc + cw_ref[j:j + 1, lanes] * sbuf[j, :, lanes]
            dst[:, lanes] = acc * _sigmoid(acc) * post_scale

    short_conv(qbuf, sq, cwq_ref, cbq_ref, qc_scr, 1.0)
    short_conv(kbuf, sk, cwk_ref, cbk_ref, kc_scr, dh ** -0.5)

    zg = zg_ref[...] + bcol_ref[...]
    zgt = zgt_ref[...] + brow_ref[...]
    lf_c = _log_sigmoid(zg)
    lf_r = _log_sigmoid(zgt)
    row_i = lax.broadcasted_iota(jnp.int32, (L, L), 0)
    col_i = lax.broadcasted_iota(jnp.int32, (L, L), 1)
    causal = row_i >= col_i
    nt_dims = (((1,), (1,)), ((), ()))
    tn_dims = (((0,), (0,)), ((), ()))
    heads = range(nh)
    cols = [slice(h * dh, (h + 1) * dh) for h in heads]

    li_col = [zg[:, h:h + 1] for h in heads]
    li_row = [zgt[h:h + 1, :] for h in heads]
    lf_row = [lf_r[nh + h:nh + h + 1, :] for h in heads]
    b_col = [jnp.sum(jnp.where(causal, lf_row[h], 0.0), axis=1, keepdims=True) for h in heads]
    b_row = [jnp.sum(jnp.where(row_i <= col_i, lf_c[:, nh + h:nh + h + 1], 0.0), axis=0, keepdims=True)
             for h in heads]
    g = [jnp.sum(lf_row[h], axis=1, keepdims=True) for h in heads]
    m_prev = [m_scr[h:h + 1, 0:1] for h in heads]

    qf = [qc_scr[:, cols[h]] for h in heads]
    kf = [kc_scr[:, cols[h]] for h in heads]
    qb = [x.astype(BF16) for x in qf]
    kb = [x.astype(BF16) for x in kf]
    vb = [v_ref[:, cols[h]].astype(BF16) for h in heads]
    qk = [lax.dot_general(qb[h], kb[h], nt_dims, preferred_element_type=F32) for h in heads]
    c_prev = [c_scr[h] for h in heads]
    n_prev = [n_scr[h:h + 1, :] for h in heads]
    qc = [jnp.dot(qb[h], c_prev[h].astype(BF16), preferred_element_type=F32) for h in heads]

    a_col = [b_col[h] + m_prev[h] for h in heads]
    logw = [jnp.where(causal, b_col[h] - b_row[h] + li_row[h], -jnp.inf) for h in heads]
    m_q = [jnp.maximum(a_col[h], jnp.max(logw[h], axis=1, keepdims=True)) for h in heads]
    s = [qk[h] * jnp.exp(logw[h] - m_q[h]) for h in heads]
    inter = [jnp.exp(a_col[h] - m_q[h]) for h in heads]
    sv = [jnp.dot(s[h].astype(BF16), vb[h], preferred_element_type=F32) for h in heads]

    logu = [g[h] - b_col[h] + li_col[h] for h in heads]
    m_new = [jnp.maximum(g[h] + m_prev[h], jnp.max(logu[h], axis=0, keepdims=True)) for h in heads]
    decay = [jnp.exp(g[h] + m_prev[h] - m_new[h]) for h in heads]
    ku = [kf[h] * jnp.exp(logu[h] - m_new[h]) for h in heads]
    kv = [lax.dot_general(ku[h].astype(BF16), vb[h], tn_dims, preferred_element_type=F32) for h in heads]
    for h in heads:
        c_scr[h] = decay[h] * c_prev[h] + kv[h]
        n_scr[h:h + 1, :] = decay[h] * n_prev[h] + jnp.sum(ku[h], axis=0, keepdims=True)
        m_scr[h:h + 1, :] = jnp.broadcast_to(m_new[h], (1, m_scr.shape[1]))

    den = [inter[h] * jnp.sum(qf[h] * n_prev[h], axis=1, keepdims=True) + jnp.sum(s[h], axis=1, keepdims=True)
           for h in heads]
    rden = [1.0 / jnp.maximum(jnp.abs(den[h]), jnp.exp(-m_q[h])) for h in heads]
    ht = [_sigmoid(o_ref[:, cols[h]]) * ((inter[h] * qc[h] + sv[h]) * rden[h]) for h in heads]
    mu = [jnp.mean(ht[h], axis=-1, keepdims=True) for h in heads]
    dv = [ht[h] - mu[h] for h in heads]
    var = [jnp.mean(dv[h] * dv[h], axis=-1, keepdims=True) for h in heads]
    for h in heads:
        out_ref[:, cols[h]] = (dv[h] * lax.rsqrt(var[h] + EPS) * ng_ref[:, cols[h]]).astype(out_ref.dtype)


def _mlstm(z, zg, zgt, qk_w, qk_b, b_i, b_f, norm_g, batch, seq):
    m = z.shape[0]
    nh = N_MLSTM_HEADS
    w = norm_g.shape[0]
    dh = w // nh
    L = CHUNK
    nc = seq // L
    blk = lambda col: pl.BlockSpec((L, w), lambda b, c: (b * nc + c, col))
    full = lambda shape: pl.BlockSpec(shape, lambda b, c: (0,) * len(shape))
    bias = jnp.concatenate([b_i, b_f]).astype(F32)
    bcol = bias.reshape(1, 2 * nh)
    brow = jnp.broadcast_to(bias[:, None], (2 * nh, L))
    return pl.pallas_call(
        functools.partial(_mlstm_kernel, nh=nh, dh=dh),
        grid=(batch, nc),
        in_specs=[blk(2), blk(3), blk(4), blk(5),
                  pl.BlockSpec((L, 2 * nh), lambda b, c: (b * nc + c, 0)),
                  pl.BlockSpec((2 * nh, L), lambda b, c: (0, b * nc + c)),
                  full((QK_CONV_WIDTH, w)), full((QK_CONV_WIDTH, w)),
                  full((1, w)), full((1, w)),
                  full((1, 2 * nh)), full((2 * nh, L)), full((1, w))],
        out_specs=pl.BlockSpec((L, w), lambda b, c: (b * nc + c, 0)),
        out_shape=jax.ShapeDtypeStruct((m, w), BF16),
        scratch_shapes=[pltpu.VMEM((QK_HALO + L, w), F32),
                        pltpu.VMEM((QK_HALO + L, w), F32),
                        pltpu.VMEM((QK_CONV_WIDTH - 1, L, w), F32),
                        pltpu.VMEM((QK_CONV_WIDTH - 1, L, w), F32),
                        pltpu.VMEM((L, w), F32),
                        pltpu.VMEM((L, w), F32),
                        pltpu.VMEM((nh, dh, dh), F32),
                        pltpu.VMEM((SUBLANES, dh), F32),
                        pltpu.VMEM((SUBLANES, LANES), F32)],
        compiler_params=_params("arbitrary", "arbitrary"),
        name="mlstm",
    )(z, z, z, z, zg, zgt, qk_w[:, :w], qk_w[:, w:], qk_b[:w].reshape(1, w), qk_b[w:].reshape(1, w),
      bcol, brow, norm_g.reshape(1, w))


def _xattn_kernel(q_ref, k_ref, v_ref, o_ref, *, nh):
    d = q_ref.shape[1]
    dh = d // nh
    scale = dh ** -0.5
    nt_dims = (((1,), (1,)), ((), ()))
    for h in range(nh):
        lo, hi = h * dh, (h + 1) * dh
        sc = lax.dot_general(q_ref[:, lo:hi], k_ref[:, lo:hi], nt_dims,
                             preferred_element_type=F32) * scale
        mx = jnp.max(sc, axis=-1, keepdims=True)
        e = jnp.exp(sc - mx)
        p = e / jnp.sum(e, axis=-1, keepdims=True)
        o_ref[:, lo:hi] = jnp.dot(p.astype(BF16), v_ref[:, lo:hi],
                                  preferred_element_type=F32).astype(o_ref.dtype)


def _xattn(q, k, v, batch, seq, mem_len, ts=512):
    m, d = q.shape
    ts = min(ts, seq)
    nt = seq // ts
    return pl.pallas_call(
        functools.partial(_xattn_kernel, nh=N_XHEADS),
        grid=(batch, nt),
        in_specs=[pl.BlockSpec((ts, d), lambda b, s: (b * nt + s, 0)),
                  pl.BlockSpec((mem_len, d), lambda b, s: (b, 0)),
                  pl.BlockSpec((mem_len, d), lambda b, s: (b, 0))],
        out_specs=pl.BlockSpec((ts, d), lambda b, s: (b * nt + s, 0)),
        out_shape=jax.ShapeDtypeStruct((m, d), BF16),
        compiler_params=_params("arbitrary", "arbitrary"),
        name="cross_attn",
    )(q, k, v)


def _swiglu_tile(h_ref, wgb_ref, wub_ref, o_ref):
    h = h_ref[...]
    g = jnp.dot(h, wgb_ref[...], preferred_element_type=F32)
    u = jnp.dot(h, wub_ref[...], preferred_element_type=F32)
    o_ref[...] = (g * _sigmoid(g) * u).astype(o_ref.dtype)


def _gateup_kernel(h_ref, wg_ref, wu_ref, o_ref, wgb_ref, wub_ref):
    @pl.when(pl.program_id(1) == 0)
    def _():
        wgb_ref[...] = wg_ref[...].astype(BF16)
        wub_ref[...] = wu_ref[...].astype(BF16)

    _swiglu_tile(h_ref, wgb_ref, wub_ref, o_ref)


def _gateup(h, wg, wu, w_lead, tm=1024, tc=512):
    m, d = h.shape
    f = wg.shape[-1]
    tm = min(tm, m)
    nlead = len(w_lead)
    wspec = pl.BlockSpec((None,) * nlead + (d, tc), lambda c, i: tuple(w_lead) + (0, c))
    return pl.pallas_call(
        _gateup_kernel,
        grid=(f // tc, m // tm),
        in_specs=[pl.BlockSpec((tm, d), lambda c, i: (i, 0)), wspec, wspec],
        out_specs=pl.BlockSpec((tm, tc), lambda c, i: (i, c)),
        out_shape=jax.ShapeDtypeStruct((m, f), BF16),
        scratch_shapes=[pltpu.VMEM((d, tc), BF16), pltpu.VMEM((d, tc), BF16)],
        compiler_params=_params("arbitrary", "arbitrary"),
        name="swiglu_gateup",
    )(h, wg, wu)


def _moe_gateup_kernel(te_ref, ta_ref, h_ref, wg_ref, wu_ref, o_ref):
    i = pl.program_id(1)

    @pl.when(ta_ref[i] > 0)
    def _():
        h = h_ref[...]
        g = jnp.dot(h, wg_ref[...].astype(BF16), preferred_element_type=F32)
        u = jnp.dot(h, wu_ref[...].astype(BF16), preferred_element_type=F32)
        o_ref[...] = (g * _sigmoid(g) * u).astype(o_ref.dtype)

    @pl.when(ta_ref[i] == 0)
    def _():
        o_ref[...] = jnp.zeros(o_ref.shape, o_ref.dtype)


def _moe_gateup(tile_e, tile_on, hs, wg, wu, layer, tc=512):
    r, d = hs.shape
    f = wg.shape[-1]
    tg = MOE_TILE
    wspec = pl.BlockSpec((None, None, d, tc), lambda c, i, te, ta: (layer, te[i], 0, c))
    return pl.pallas_call(
        _moe_gateup_kernel,
        grid_spec=pltpu.PrefetchScalarGridSpec(
            num_scalar_prefetch=2,
            grid=(f // tc, r // tg),
            in_specs=[pl.BlockSpec((tg, d), lambda c, i, te, ta: (i, 0)), wspec, wspec],
            out_specs=pl.BlockSpec((tg, tc), lambda c, i, te, ta: (i, c))),
        out_shape=jax.ShapeDtypeStruct((r, f), BF16),
        compiler_params=_params("arbitrary", "arbitrary"),
        name="moe_gateup",
    )(tile_e, tile_on, hs, wg, wu)


def _moe_down_kernel(te_ref, ta_ref, a_ref, w_ref, o_ref):
    i = pl.program_id(1)

    @pl.when(ta_ref[i] > 0)
    def _():
        o_ref[...] = jnp.dot(a_ref[...], w_ref[...].astype(BF16), preferred_element_type=F32)

    @pl.when(ta_ref[i] == 0)
    def _():
        o_ref[...] = jnp.zeros(o_ref.shape, o_ref.dtype)


def _moe_down(tile_e, tile_on, a, wd, layer, tn=512):
    r, f = a.shape
    d = wd.shape[-1]
    tg = MOE_TILE
    return pl.pallas_call(
        _moe_down_kernel,
        grid_spec=pltpu.PrefetchScalarGridSpec(
            num_scalar_prefetch=2,
            grid=(d // tn, r // tg),
            in_specs=[pl.BlockSpec((tg, f), lambda j, i, te, ta: (i, 0)),
                      pl.BlockSpec((None, None, f, tn), lambda j, i, te, ta: (layer, te[i], 0, j))],
            out_specs=pl.BlockSpec((tg, tn), lambda j, i, te, ta: (i, j))),
        out_shape=jax.ShapeDtypeStruct((r, d), F32),
        compiler_params=_params("arbitrary", "arbitrary"),
        name="moe_down",
    )(tile_e, tile_on, a, wd)


def _router_kernel(h_ref, wr_ref, br_ref, o_ref, *, ne):
    logits = jnp.dot(h_ref[...], wr_ref[...].astype(BF16), preferred_element_type=F32) + br_ref[...]
    lane = lax.broadcasted_iota(jnp.int32, logits.shape, 1).astype(F32)
    lg = jnp.where(lane < ne, logits, -jnp.inf)
    v1 = jnp.max(lg, axis=1, keepdims=True)
    i1 = jnp.min(jnp.where(lg == v1, lane, float(LANES)), axis=1, keepdims=True)
    lg2 = jnp.where(lane == i1, -jnp.inf, lg)
    v2 = jnp.max(lg2, axis=1, keepdims=True)
    i2 = jnp.min(jnp.where(lg2 == v2, lane, float(LANES)), axis=1, keepdims=True)
    e2 = jnp.exp(v2 - v1)
    w1 = 1.0 / (1.0 + e2)
    w2 = e2 / (1.0 + e2)
    o_ref[...] = jnp.where(lane == 0, i1, jnp.where(lane == 1, i2,
                           jnp.where(lane == 2, w1, jnp.where(lane == 3, w2, 0.0))))


def _router(h, w_r, b_r, tm=1024):
    m, d = h.shape
    ne = w_r.shape[1]
    tm = min(tm, m)
    wr = jnp.zeros((d, LANES), F32).at[:, :ne].set(w_r)
    br = jnp.zeros((1, LANES), F32).at[0, :ne].set(b_r)
    return pl.pallas_call(
        functools.partial(_router_kernel, ne=ne),
        grid=(m // tm,),
        in_specs=[pl.BlockSpec((tm, d), lambda i: (i, 0)),
                  pl.BlockSpec((d, LANES), lambda i: (0, 0)),
                  pl.BlockSpec((1, LANES), lambda i: (0, 0))],
        out_specs=pl.BlockSpec((tm, LANES), lambda i: (i, 0)),
        out_shape=jax.ShapeDtypeStruct((m, LANES), F32),
        compiler_params=_params("arbitrary"),
        name="moe_router",
    )(h, wr, br)


def _row_copy(src_hbm, row, buf, slot, sem):
    return pltpu.make_async_copy(src_hbm.at[pl.ds(row, 1), :], buf.at[pl.ds(slot, 1), :], sem)


def _dispatch_kernel(src_ref, nrows_ref, x_hbm, g_ref, o_ref, buf, sem, *, rows):
    base = pl.program_id(0) * rows
    active = base < nrows_ref[0]

    @pl.when(active)
    def _():
        def start(c, carry):
            for u in range(DMA_UNROLL):
                r = u * (rows // DMA_UNROLL) + c
                _row_copy(x_hbm, src_ref[base + r], buf, r, sem).start(priority=u % 2)
            return carry

        def wait(c, carry):
            for u in range(DMA_UNROLL):
                _row_copy(x_hbm, 0, buf, c * DMA_UNROLL + u, sem).wait()
            return carry

        lax.fori_loop(0, rows // DMA_UNROLL, start, 0)
        lax.fori_loop(0, rows // DMA_UNROLL, wait, 0)
        x = buf[...]
        ms = jnp.mean(x * x, axis=-1, keepdims=True)
        o_ref[...] = (x * lax.rsqrt(ms + EPS) * g_ref[...]).astype(o_ref.dtype)

    @pl.when(jnp.logical_not(active))
    def _():
        o_ref[...] = jnp.zeros(o_ref.shape, o_ref.dtype)


def _dispatch(src, n_used, x, g, n_rows):
    m, d = x.shape
    rows = GATHER_ROWS
    return pl.pallas_call(
        functools.partial(_dispatch_kernel, rows=rows),
        grid_spec=pltpu.PrefetchScalarGridSpec(
            num_scalar_prefetch=2,
            grid=(n_rows // rows,),
            in_specs=[pl.BlockSpec(memory_space=pl.ANY),
                      pl.BlockSpec((1, d), lambda i, s, n: (0, 0))],
            out_specs=pl.BlockSpec((rows, d), lambda i, s, n: (i, 0)),
            scratch_shapes=[pltpu.VMEM((rows, d), F32), pltpu.SemaphoreType.DMA(())]),
        out_shape=jax.ShapeDtypeStruct((n_rows, d), BF16),
        compiler_params=_params("arbitrary"),
        name="moe_dispatch",
    )(src, n_used, x, g.reshape(1, d))


def _combine_kernel(p1_ref, p2_ref, x_ref, route_ref, g_ref, y_hbm, *rest, rows, emit_x):
    if emit_x:
        o_ref, hn_ref, buf1, buf2, sem = rest
    else:
        hn_ref, buf1, buf2, sem = rest
    base = pl.program_id(0) * rows

    def start(c, carry):
        for u in range(DMA_UNROLL):
            r = c * DMA_UNROLL + u
            _row_copy(y_hbm, p1_ref[base + r], buf1, r, sem.at[0]).start(priority=0)
            _row_copy(y_hbm, p2_ref[base + r], buf2, r, sem.at[1]).start(priority=1)
        return carry

    def wait(c, carry):
        for u in range(DMA_UNROLL):
            r = c * DMA_UNROLL + u
            _row_copy(y_hbm, 0, buf1, r, sem.at[0]).wait()
            _row_copy(y_hbm, 0, buf2, r, sem.at[1]).wait()
        return carry

    lax.fori_loop(0, rows // DMA_UNROLL, start, 0)
    lax.fori_loop(0, rows // DMA_UNROLL, wait, 0)
    route = route_ref[...]
    out = x_ref[...] + route[:, 2:3] * buf1[...] + route[:, 3:4] * buf2[...]
    if emit_x:
        o_ref[...] = out
    ms = jnp.mean(out * out, axis=-1, keepdims=True)
    hn_ref[...] = (out * lax.rsqrt(ms + EPS) * g_ref[...]).astype(hn_ref.dtype)


def _combine(p1, p2, x, route, y, next_gain, next_dtype, emit_x):
    m, d = x.shape
    rows = GATHER_ROWS
    tile = pl.BlockSpec((rows, d), lambda i, a, b: (i, 0))
    out_specs = [tile, tile] if emit_x else [tile]
    out_shape = [jax.ShapeDtypeStruct((m, d), next_dtype)]
    if emit_x:
        out_shape.insert(0, jax.ShapeDtypeStruct((m, d), F32))
    res = pl.pallas_call(
        functools.partial(_combine_kernel, rows=rows, emit_x=emit_x),
        grid_spec=pltpu.PrefetchScalarGridSpec(
            num_scalar_prefetch=2,
            grid=(m // rows,),
            in_specs=[tile,
                      pl.BlockSpec((rows, LANES), lambda i, a, b: (i, 0)),
                      pl.BlockSpec((1, d), lambda i, a, b: (0, 0)),
                      pl.BlockSpec(memory_space=pl.ANY)],
            out_specs=out_specs,
            scratch_shapes=[pltpu.VMEM((rows, d), F32), pltpu.VMEM((rows, d), F32),
                            pltpu.SemaphoreType.DMA((2,))]),
        out_shape=out_shape,
        compiler_params=_params("arbitrary"),
        name="moe_combine",
    )(p1, p2, x, route, next_gain.reshape(1, d), y)
    return (res[0], res[1]) if emit_x else (None, res[0])


def _moe_plan(route, m, ne, tg):
    i1 = route[:, 0].astype(jnp.int32)
    i2 = route[:, 1].astype(jnp.int32)
    experts = jnp.arange(ne, dtype=jnp.int32)
    sel = ((i1[:, None] == experts) | (i2[:, None] == experts)).astype(jnp.int32)
    counts = jnp.sum(sel, axis=0)
    padded = ((counts + tg - 1) // tg) * tg
    ends = jnp.cumsum(padded)
    pos = (ends - padded)[None, :] + jnp.cumsum(sel, axis=0) - sel
    p1 = jnp.take_along_axis(pos, i1[:, None], axis=1)[:, 0]
    p2 = jnp.take_along_axis(pos, i2[:, None], axis=1)[:, 0]
    n_tiles = (2 * m) // tg + ne
    tok = jnp.arange(m, dtype=jnp.int32)
    filler = jnp.arange(n_tiles * tg, dtype=jnp.int32) % m
    src = filler.at[jnp.concatenate([p1, p2])].set(jnp.concatenate([tok, tok]))
    tile_start = jnp.arange(n_tiles, dtype=jnp.int32) * tg
    tile_e = jnp.minimum(jnp.sum((tile_start[:, None] >= ends[None, :]).astype(jnp.int32), axis=1), ne - 1)
    tile_on = (tile_start < ends[-1]).astype(jnp.int32)
    return p1, p2, src, ends[-1:], tile_e, tile_on, n_tiles * tg


def _moe_block(x, h, g_ffn, w_r, b_r, wg, wu, wd, layer, next_gain, next_dtype, emit_x):
    m, d = x.shape
    route = _router(h, w_r, b_r)
    p1, p2, src, n_used, tile_e, tile_on, n_rows = _moe_plan(route, m, N_EXPERTS, MOE_TILE)
    hs = _dispatch(src, n_used, x, g_ffn, n_rows)
    a = _moe_gateup(tile_e, tile_on, hs, wg, wu, layer)
    y = _moe_down(tile_e, tile_on, a, wd, layer)
    return _combine(p1, p2, x, route, y, next_gain, next_dtype, emit_x)


def kernel(x, mem, norm_mix, w_in, conv_dw_w, conv_dw_b, conv_ln_g, conv_ln_b, qk_conv_w, qk_conv_b, b_igate, b_fgate, mlstm_norm_g, w_out, norm_cross, norm_mem, w_cq, w_ck, w_cv, w_co, norm_ffn, w_gate_dense, w_up_dense, w_down_dense, w_router, b_router, w_gate_moe, w_up_moe, w_down_moe, norm_final):
    batch, seq, d = x.shape
    mem_len = mem.shape[1]
    depth = norm_mix.shape[0]
    d_conv = conv_dw_w.shape[2]
    d_mlstm = mlstm_norm_g.shape[1]
    n_main = 2 * d_conv + 4 * d_mlstm
    nh = N_MLSTM_HEADS
    xf = x.reshape(batch * seq, d)
    memf = mem.reshape(batch * mem_len, d)

    w_in_nk = jnp.swapaxes(w_in, 1, 2)
    h = _rmsnorm(xf, norm_mix[0], BF16)
    for l in range(depth):
        last = l == depth - 1
        next_gain = norm_final if last else norm_mix[l + 1]
        next_dtype = F32 if last else BF16
        z = _matmul([h], w_in_nk, (l,), n_cols=n_main, w_is_nk=True, tn=1024, name="w_in")
        zgt = _gates(h, w_in_nk, l, n_main, 2 * nh)
        yc = _conformer(z, conv_dw_w[l], conv_dw_b[l], conv_ln_g[l], conv_ln_b[l], batch, seq)
        ym = _mlstm(z, zgt.T, zgt, qk_conv_w[l], qk_conv_b[l], b_igate[l], b_fgate[l], mlstm_norm_g[l],
                    batch, seq)
        xf, h = _matmul([yc, ym], w_out, (l,), residual=xf, norm_gain=norm_cross[l], tm=512, tn=d,
                        name="w_out")
        mem_n = _rmsnorm(memf, norm_mem[l], BF16)
        q = _matmul([h], w_cq, (l,), out_dtype=BF16, tn=1024, name="w_cq")
        kk = _matmul([mem_n], w_ck, (l,), out_dtype=BF16, name="w_ck")
        vv = _matmul([mem_n], w_cv, (l,), out_dtype=BF16, name="w_cv")
        att = _xattn(q, kk, vv, batch, seq, mem_len)
        xf, h = _matmul([att], w_co, (l,), residual=xf, norm_gain=norm_ffn[l], tm=512, tn=d, name="w_co")
        j = l // 2
        if l % 2 == 0:
            a = _gateup(h, w_gate_dense, w_up_dense, (j,))
            xf = _matmul([a], w_down_dense, (j,), residual=xf, tm=512, tn=512, name="w_down")
            h = _rmsnorm(xf, next_gain, next_dtype)
        else:
            xf, h = _moe_block(xf, h, norm_ffn[l], w_router[j], b_router[j], w_gate_moe, w_up_moe,
                               w_down_moe, j, next_gain, next_dtype, emit_x=not last)
    return h.reshape(batch, seq, d)
```

```python
import functools

import jax
import jax.numpy as jnp
from jax import lax
from jax.experimental import pallas as pl
from jax.experimental.pallas import tpu as pltpu

F32 = jnp.float32
BF16 = jnp.bfloat16
EPS = 1e-6

V7X_VMEM_BYTES = 64 * 1024 * 1024
VMEM_LIMIT = V7X_VMEM_BYTES - 8 * 1024 * 1024
LANES = 128
SUBLANES = 8

N_MLSTM_HEADS = 4
N_XHEADS = 4
CONV_WIDTH = 31
QK_CONV_WIDTH = 4
CHUNK = 128
N_EXPERTS = 8
MOE_TILE = 512
GATHER_ROWS = 512
DMA_UNROLL = 8


def _params(*sem):
    return pltpu.CompilerParams(dimension_semantics=sem, vmem_limit_bytes=VMEM_LIMIT)


def _sigmoid(x):
    return 1.0 / (1.0 + jnp.exp(-x))


def _log_sigmoid(x):
    return jnp.minimum(x, 0.0) - jnp.log(1.0 + jnp.exp(-jnp.abs(x)))


def _rmsnorm_kernel(x_ref, g_ref, o_ref):
    x = x_ref[...]
    ms = jnp.mean(x * x, axis=-1, keepdims=True)
    o_ref[...] = (x * lax.rsqrt(ms + EPS) * g_ref[...]).astype(o_ref.dtype)


def _rmsnorm(x, g, out_dtype, tm=512):
    m, d = x.shape
    tm = min(tm, m)
    return pl.pallas_call(
        _rmsnorm_kernel,
        grid=(m // tm,),
        in_specs=[pl.BlockSpec((tm, d), lambda i: (i, 0)),
                  pl.BlockSpec((1, d), lambda i: (0, 0))],
        out_specs=pl.BlockSpec((tm, d), lambda i: (i, 0)),
        out_shape=jax.ShapeDtypeStruct((m, d), out_dtype),
        compiler_params=_params("arbitrary"),
        name="rmsnorm",
    )(x, g.reshape(1, d))


NT_DIMS = (((1,), (1,)), ((), ()))


def _matmul_kernel(*refs, k_sizes, w_is_nk, has_res, has_norm):
    na = len(k_sizes)
    a_refs = refs[:na]
    w_ref = refs[na]
    pos = na + 1
    r_ref = g_ref = hn_ref = None
    if has_res:
        r_ref = refs[pos]
        pos += 1
    if has_norm:
        g_ref = refs[pos]
        pos += 1
    o_ref = refs[pos]
    if has_norm:
        hn_ref = refs[pos + 1]
    wb_ref = refs[-1]

    @pl.when(pl.program_id(1) == 0)
    def _():
        wb_ref[...] = w_ref[...].astype(BF16)

    acc = None
    k0 = 0
    for a_ref, ks in zip(a_refs, k_sizes):
        if w_is_nk:
            part = lax.dot_general(a_ref[...], wb_ref[:, k0:k0 + ks], NT_DIMS, preferred_element_type=F32)
        else:
            part = jnp.dot(a_ref[...], wb_ref[k0:k0 + ks, :], preferred_element_type=F32)
        acc = part if acc is None else acc + part
        k0 += ks
    if has_res:
        acc = acc + r_ref[...]
    o_ref[...] = acc.astype(o_ref.dtype)
    if has_norm:
        ms = jnp.mean(acc * acc, axis=-1, keepdims=True)
        hn_ref[...] = (acc * lax.rsqrt(ms + EPS) * g_ref[...]).astype(hn_ref.dtype)


def _matmul(a_list, w, w_lead=(), *, n_cols=None, w_is_nk=False, residual=None, norm_gain=None,
            out_dtype=F32, tm=1024, tn=512, name="matmul"):
    m = a_list[0].shape[0]
    k_sizes = tuple(a.shape[1] for a in a_list)
    k = sum(k_sizes)
    n_total = w.shape[-2] if w_is_nk else w.shape[-1]
    n = n_total if n_cols is None else n_cols
    tm = min(tm, m)
    tn = min(tn, n)
    nlead = len(w_lead)
    lead = tuple(w_lead)
    if w_is_nk:
        w_block, w_map, wb_shape = (None,) * nlead + (tn, k), (lambda j, i: lead + (j, 0)), (tn, k)
    else:
        w_block, w_map, wb_shape = (None,) * nlead + (k, tn), (lambda j, i: lead + (0, j)), (k, tn)
    w_mode = dict(pipeline_mode=pl.Buffered(1)) if n == tn else {}
    in_specs = [pl.BlockSpec((tm, ks), lambda j, i: (i, 0)) for ks in k_sizes]
    in_specs.append(pl.BlockSpec(w_block, w_map, **w_mode))
    args = list(a_list) + [w]
    out_tile = pl.BlockSpec((tm, tn), lambda j, i: (i, j))
    if residual is not None:
        in_specs.append(out_tile)
        args.append(residual)
    out_specs, out_shape = out_tile, jax.ShapeDtypeStruct((m, n), out_dtype)
    if norm_gain is not None:
        assert tn == n, "the fused RMSNorm needs whole rows"
        in_specs.append(pl.BlockSpec((1, n), lambda j, i: (0, 0)))
        args.append(norm_gain.reshape(1, n))
        out_specs, out_shape = [out_tile, out_tile], [out_shape, jax.ShapeDtypeStruct((m, n), BF16)]
    return pl.pallas_call(
        functools.partial(_matmul_kernel, k_sizes=k_sizes, w_is_nk=w_is_nk,
                          has_res=residual is not None, has_norm=norm_gain is not None),
        grid=(n // tn, m // tm),
        in_specs=in_specs,
        out_specs=out_specs,
        out_shape=out_shape,
        scratch_shapes=[pltpu.VMEM(wb_shape, BF16)],
        compiler_params=_params("arbitrary", "arbitrary"),
        name=name,
    )(*args)


def _gates_kernel(h_ref, w_ref, o_ref):
    o_ref[...] = lax.dot_general(w_ref[...].astype(BF16), h_ref[...], NT_DIMS, preferred_element_type=F32)


def _gates(h, w_nk, lead, row0, nrows, tm=1024):
    m, k = h.shape
    tm = min(tm, m)
    assert row0 % nrows == 0
    return pl.pallas_call(
        _gates_kernel,
        grid=(m // tm,),
        in_specs=[pl.BlockSpec((tm, k), lambda i: (i, 0)),
                  pl.BlockSpec((None, nrows, k), lambda i: (lead, row0 // nrows, 0))],
        out_specs=pl.BlockSpec((nrows, tm), lambda i: (0, i)),
        out_shape=jax.ShapeDtypeStruct((nrows, m), F32),
        compiler_params=_params("arbitrary"),
        name="w_in_gates",
    )(h, w_nk)


CONV_HALO = 32
CONV_BLOCK = 64
CONV_FIRST = CONV_HALO - (CONV_WIDTH - 1)
CONV_LANES = 512


def _conformer_kernel(a_ref, g_ref, w_ref, b_ref, lng_ref, lnb_ref, o_ref, ubuf, sbuf, cbuf, wrep, *, ts):
    s = pl.program_id(1)
    c = a_ref.shape[1]

    @pl.when(jnp.logical_and(pl.program_id(0) == 0, s == 0))
    def _():
        for k in range(CONV_WIDTH):
            wrep[k * SUBLANES:(k + 1) * SUBLANES, :] = jnp.broadcast_to(w_ref[k:k + 1, :], (SUBLANES, c))

    @pl.when(s == 0)
    def _():
        ubuf[0:CONV_HALO, :] = jnp.zeros((CONV_HALO, c), F32)

    @pl.when(s > 0)
    def _():
        ubuf[0:CONV_HALO, :] = ubuf[ts:ts + CONV_HALO, :]

    ubuf[CONV_HALO:CONV_HALO + ts, :] = a_ref[...] * _sigmoid(g_ref[...])
    for sft in range(1, SUBLANES):
        sbuf[sft - 1] = ubuf[sft:sft + sbuf.shape[1], :]
    lng = lng_ref[...]
    lnb = lnb_ref[...]
    nsub = CONV_BLOCK // SUBLANES
    def rows8(start):
        return pl.ds(pl.multiple_of(start, SUBLANES), SUBLANES)

    def row_block(r, carry):
        base = r * CONV_BLOCK
        for lc in range(c // CONV_LANES):
            lanes = slice(lc * CONV_LANES, (lc + 1) * CONV_LANES)
            accs = [jnp.broadcast_to(b_ref[:, lanes], (SUBLANES, CONV_LANES))] * nsub
            for k in range(CONV_WIDTH):
                sft = (CONV_FIRST + k) % SUBLANES
                row = base + (CONV_FIRST + k) - sft
                wk = wrep[k * SUBLANES:(k + 1) * SUBLANES, lanes]
                for i in range(nsub):
                    lo = rows8(row + i * SUBLANES)
                    tap = ubuf[lo, lanes] if sft == 0 else sbuf[sft - 1, lo, lanes]
                    accs[i] = accs[i] + wk * tap
            for i in range(nsub):
                cbuf[rows8(base + i * SUBLANES), lanes] = accs[i]
        blk = pl.ds(pl.multiple_of(base, CONV_BLOCK), CONV_BLOCK)
        y = cbuf[blk, :]
        mu = jnp.mean(y, axis=-1, keepdims=True)
        d = y - mu
        var = jnp.mean(d * d, axis=-1, keepdims=True)
        y = d * lax.rsqrt(var + EPS) * lng + lnb
        o_ref[blk, :] = (y * _sigmoid(y)).astype(o_ref.dtype)
        return carry

    lax.fori_loop(0, ts // CONV_BLOCK, row_block, 0)


def _conformer(z, w_dw, b_dw, ln_g, ln_b, batch, seq, ts=512):
    m = z.shape[0]
    c = w_dw.shape[1]
    ts = min(ts, seq)
    nt = seq // ts
    row = lambda b, s: (b * nt + s, 0)
    vec = pl.BlockSpec((1, c), lambda b, s: (0, 0))
    return pl.pallas_call(
        functools.partial(_conformer_kernel, ts=ts),
        grid=(batch, nt),
        in_specs=[pl.BlockSpec((ts, c), row),
                  pl.BlockSpec((ts, c), lambda b, s: (b * nt + s, 1)),
                  pl.BlockSpec((CONV_WIDTH, c), lambda b, s: (0, 0)),
                  vec, vec, vec],
        out_specs=pl.BlockSpec((ts, c), row),
        out_shape=jax.ShapeDtypeStruct((m, c), BF16),
        scratch_shapes=[pltpu.VMEM((CONV_HALO + ts, c), F32),
                        pltpu.VMEM((SUBLANES - 1, ts + CONV_HALO - SUBLANES, c), F32),
                        pltpu.VMEM((ts, c), F32),
                        pltpu.VMEM((CONV_WIDTH * SUBLANES, c), F32)],
        compiler_params=_params("arbitrary", "arbitrary"),
        name="conformer_conv",
    )(z, z, w_dw, b_dw.reshape(1, c), ln_g.reshape(1, c), ln_b.reshape(1, c))


QK_HALO = 8


def _mlstm_kernel(q_ref, k_ref, v_ref, o_ref, zg_ref, zgt_ref, cwq_ref, cwk_ref, cbq_ref, cbk_ref,
                  bcol_ref, brow_ref, ng_ref, out_ref, qbuf, kbuf, sq, sk, qc_scr, kc_scr, c_scr, n_scr, m_scr,
                  *, nh, dh):
    ci = pl.program_id(1)
    L = q_ref.shape[0]
    w = q_ref.shape[1]

    @pl.when(ci == 0)
    def _():
        qbuf[0:QK_HALO, :] = jnp.zeros((QK_HALO, w), F32)
        kbuf[0:QK_HALO, :] = jnp.zeros((QK_HALO, w), F32)
        c_scr[...] = jnp.zeros(c_scr.shape, F32)
        n_scr[...] = jnp.zeros(n_scr.shape, F32)
        m_scr[...] = jnp.zeros(m_scr.shape, F32)

    @pl.when(ci > 0)
    def _():
        qbuf[0:QK_HALO, :] = qbuf[L:L + QK_HALO, :]
        kbuf[0:QK_HALO, :] = kbuf[L:L + QK_HALO, :]

    qbuf[QK_HALO:QK_HALO + L, :] = q_ref[...]
    kbuf[QK_HALO:QK_HALO + L, :] = k_ref[...]

    def short_conv(buf, sbuf, cw_ref, cb_ref, dst, post_scale):
        last = QK_CONV_WIDTH - 1
        for j in range(last):
            off = QK_HALO - last + j
            sbuf[j] = buf[off:off + L, :]
        for lc in range(w // LANES):
            lanes = slice(lc * LANES, (lc + 1) * LANES)
            acc = cb_ref[:, lanes] + cw_ref[last:last + 1, lanes] * buf[QK_HALO:QK_HALO + L, lanes]
            for j in range(last):
                acc = acc + cw_ref[j:j + 1, lanes] * sbuf[j, :, lanes]
            dst[:, lanes] = acc * _sigmoid(acc) * post_scale

    short_conv(qbuf, sq, cwq_ref, cbq_ref, qc_scr, 1.0)
    short_conv(kbuf, sk, cwk_ref, cbk_ref, kc_scr, dh ** -0.5)

    zg = zg_ref[...] + bcol_ref[...]
    zgt = zgt_ref[...] + brow_ref[...]
    lf_c = _log_sigmoid(zg)
    lf_r = _log_sigmoid(zgt)
    row_i = lax.broadcasted_iota(jnp.int32, (L, L), 0)
    col_i = lax.broadcasted_iota(jnp.int32, (L, L), 1)
    causal = row_i >= col_i
    nt_dims = (((1,), (1,)), ((), ()))
    tn_dims = (((0,), (0,)), ((), ()))
    heads = range(nh)
    cols = [slice(h * dh, (h + 1) * dh) for h in heads]

    li_col = [zg[:, h:h + 1] for h in heads]
    li_row = [zgt[h:h + 1, :] for h in heads]
    lf_row = [lf_r[nh + h:nh + h + 1, :] for h in heads]
    b_col = [jnp.sum(jnp.where(causal, lf_row[h], 0.0), axis=1, keepdims=True) for h in heads]
    b_row = [jnp.sum(jnp.where(row_i <= col_i, lf_c[:, nh + h:nh + h + 1], 0.0), axis=0, keepdims=True)
             for h in heads]
    g = [jnp.sum(lf_row[h], axis=1, keepdims=True) for h in heads]
    m_prev = [m_scr[h:h + 1, 0:1] for h in heads]

    qf = [qc_scr[:, cols[h]] for h in heads]
    kf = [kc_scr[:, cols[h]] for h in heads]
    qb = [x.astype(BF16) for x in qf]
    kb = [x.astype(BF16) for x in kf]
    vb = [v_ref[:, cols[h]].astype(BF16) for h in heads]
    qk = [lax.dot_general(qb[h], kb[h], nt_dims, preferred_element_type=F32) for h in heads]
    c_prev = [c_scr[h] for h in heads]
    n_prev = [n_scr[h:h + 1, :] for h in heads]
    qc = [jnp.dot(qb[h], c_prev[h].astype(BF16), preferred_element_type=F32) for h in heads]

    a_col = [b_col[h] + m_prev[h] for h in heads]
    logw = [jnp.where(causal, b_col[h] - b_row[h] + li_row[h], -jnp.inf) for h in heads]
    m_q = [jnp.maximum(a_col[h], jnp.max(logw[h], axis=1, keepdims=True)) for h in heads]
    s = [qk[h] * jnp.exp(logw[h] - m_q[h]) for h in heads]
    inter = [jnp.exp(a_col[h] - m_q[h]) for h in heads]
    sv = [jnp.dot(s[h].astype(BF16), vb[h], preferred_element_type=F32) for h in heads]

    logu = [g[h] - b_col[h] + li_col[h] for h in heads]
    m_new = [jnp.maximum(g[h] + m_prev[h], jnp.max(logu[h], axis=0, keepdims=True)) for h in heads]
    decay = [jnp.exp(g[h] + m_prev[h] - m_new[h]) for h in heads]
    ku = [kf[h] * jnp.exp(logu[h] - m_new[h]) for h in heads]
    kv = [lax.dot_general(ku[h].astype(BF16), vb[h], tn_dims, preferred_element_type=F32) for h in heads]
    for h in heads:
        c_scr[h] = decay[h] * c_prev[h] + kv[h]
        n_scr[h:h + 1, :] = decay[h] * n_prev[h] + jnp.sum(ku[h], axis=0, keepdims=True)
        m_scr[h:h + 1, :] = jnp.broadcast_to(m_new[h], (1, m_scr.shape[1]))

    den = [inter[h] * jnp.sum(qf[h] * n_prev[h], axis=1, keepdims=True) + jnp.sum(s[h], axis=1, keepdims=True)
           for h in heads]
    rden = [1.0 / jnp.maximum(jnp.abs(den[h]), jnp.exp(-m_q[h])) for h in heads]
    ht = [_sigmoid(o_ref[:, cols[h]]) * ((inter[h] * qc[h] + sv[h]) * rden[h]) for h in heads]
    mu = [jnp.mean(ht[h], axis=-1, keepdims=True) for h in heads]
    dv = [ht[h] - mu[h] for h in heads]
    var = [jnp.mean(dv[h] * dv[h], axis=-1, keepdims=True) for h in heads]
    for h in heads:
        out_ref[:, cols[h]] = (dv[h] * lax.rsqrt(var[h] + EPS) * ng_ref[:, cols[h]]).astype(out_ref.dtype)


def _mlstm(z, zg, zgt, qk_w, qk_b, b_i, b_f, norm_g, batch, seq):
    m = z.shape[0]
    nh = N_MLSTM_HEADS
    w = norm_g.shape[0]
    dh = w // nh
    L = CHUNK
    nc = seq // L
    blk = lambda col: pl.BlockSpec((L, w), lambda b, c: (b * nc + c, col))
    full = lambda shape: pl.BlockSpec(shape, lambda b, c: (0,) * len(shape))
    bias = jnp.concatenate([b_i, b_f]).astype(F32)
    bcol = bias.reshape(1, 2 * nh)
    brow = jnp.broadcast_to(bias[:, None], (2 * nh, L))
    return pl.pallas_call(
        functools.partial(_mlstm_kernel, nh=nh, dh=dh),
        grid=(batch, nc),
        in_specs=[blk(2), blk(3), blk(4), blk(5),
                  pl.BlockSpec((L, 2 * nh), lambda b, c: (b * nc + c, 0)),
                  pl.BlockSpec((2 * nh, L), lambda b, c: (0, b * nc + c)),
                  full((QK_CONV_WIDTH, w)), full((QK_CONV_WIDTH, w)),
                  full((1, w)), full((1, w)),
                  full((1, 2 * nh)), full((2 * nh, L)), full((1, w))],
        out_specs=pl.BlockSpec((L, w), lambda b, c: (b * nc + c, 0)),
        out_shape=jax.ShapeDtypeStruct((m, w), BF16),
        scratch_shapes=[pltpu.VMEM((QK_HALO + L, w), F32),
                        pltpu.VMEM((QK_HALO + L, w), F32),
                        pltpu.VMEM((QK_CONV_WIDTH - 1, L, w), F32),
                        pltpu.VMEM((QK_CONV_WIDTH - 1, L, w), F32),
                        pltpu.VMEM((L, w), F32),
                        pltpu.VMEM((L, w), F32),
                        pltpu.VMEM((nh, dh, dh), F32),
                        pltpu.VMEM((SUBLANES, dh), F32),
                        pltpu.VMEM((SUBLANES, LANES), F32)],
        compiler_params=_params("arbitrary", "arbitrary"),
        name="mlstm",
    )(z, z, z, z, zg, zgt, qk_w[:, :w], qk_w[:, w:], qk_b[:w].reshape(1, w), qk_b[w:].reshape(1, w),
      bcol, brow, norm_g.reshape(1, w))


def _xattn_kernel(q_ref, k_ref, v_ref, o_ref, *, nh):
    d = q_ref.shape[1]
    dh = d // nh
    scale = dh ** -0.5
    nt_dims = (((1,), (1,)), ((), ()))
    for h in range(nh):
        lo, hi = h * dh, (h + 1) * dh
        sc = lax.dot_general(q_ref[:, lo:hi], k_ref[:, lo:hi], nt_dims,
                             preferred_element_type=F32) * scale
        mx = jnp.max(sc, axis=-1, keepdims=True)
        e = jnp.exp(sc - mx)
        p = e / jnp.sum(e, axis=-1, keepdims=True)
        o_ref[:, lo:hi] = jnp.dot(p.astype(BF16), v_ref[:, lo:hi],
                                  preferred_element_type=F32).astype(o_ref.dtype)


def _xattn(q, k, v, batch, seq, mem_len, ts=512):
    m, d = q.shape
    ts = min(ts, seq)
    nt = seq // ts
    return pl.pallas_call(
        functools.partial(_xattn_kernel, nh=N_XHEADS),
        grid=(batch, nt),
        in_specs=[pl.BlockSpec((ts, d), lambda b, s: (b * nt + s, 0)),
                  pl.BlockSpec((mem_len, d), lambda b, s: (b, 0)),
                  pl.BlockSpec((mem_len, d), lambda b, s: (b, 0))],
        out_specs=pl.BlockSpec((ts, d), lambda b, s: (b * nt + s, 0)),
        out_shape=jax.ShapeDtypeStruct((m, d), BF16),
        compiler_params=_params("arbitrary", "arbitrary"),
        name="cross_attn",
    )(q, k, v)


def _swiglu_tile(h_ref, wgb_ref, wub_ref, o_ref):
    h = h_ref[...]
    g = jnp.dot(h, wgb_ref[...], preferred_element_type=F32)
    u = jnp.dot(h, wub_ref[...], preferred_element_type=F32)
    o_ref[...] = (g * _sigmoid(g) * u).astype(o_ref.dtype)


def _gateup_kernel(h_ref, wg_ref, wu_ref, o_ref, wgb_ref, wub_ref):
    @pl.when(pl.program_id(1) == 0)
    def _():
        wgb_ref[...] = wg_ref[...].astype(BF16)
        wub_ref[...] = wu_ref[...].astype(BF16)

    _swiglu_tile(h_ref, wgb_ref, wub_ref, o_ref)


def _gateup(h, wg, wu, w_lead, tm=1024, tc=512):
    m, d = h.shape
    f = wg.shape[-1]
    tm = min(tm, m)
    nlead = len(w_lead)
    wspec = pl.BlockSpec((None,) * nlead + (d, tc), lambda c, i: tuple(w_lead) + (0, c))
    return pl.pallas_call(
        _gateup_kernel,
        grid=(f // tc, m // tm),
        in_specs=[pl.BlockSpec((tm, d), lambda c, i: (i, 0)), wspec, wspec],
        out_specs=pl.BlockSpec((tm, tc), lambda c, i: (i, c)),
        out_shape=jax.ShapeDtypeStruct((m, f), BF16),
        scratch_shapes=[pltpu.VMEM((d, tc), BF16), pltpu.VMEM((d, tc), BF16)],
        compiler_params=_params("arbitrary", "arbitrary"),
        name="swiglu_gateup",
    )(h, wg, wu)


def _moe_gateup_kernel(te_ref, ta_ref, h_ref, wg_ref, wu_ref, o_ref):
    i = pl.program_id(1)

    @pl.when(ta_ref[i] > 0)
    def _():
        h = h_ref[...]
        g = jnp.dot(h, wg_ref[...].astype(BF16), preferred_element_type=F32)
        u = jnp.dot(h, wu_ref[...].astype(BF16), preferred_element_type=F32)
        o_ref[...] = (g * _sigmoid(g) * u).astype(o_ref.dtype)

    @pl.when(ta_ref[i] == 0)
    def _():
        o_ref[...] = jnp.zeros(o_ref.shape, o_ref.dtype)


def _moe_gateup(tile_e, tile_on, hs, wg, wu, layer, tc=512):
    r, d = hs.shape
    f = wg.shape[-1]
    tg = MOE_TILE
    wspec = pl.BlockSpec((None, None, d, tc), lambda c, i, te, ta: (layer, te[i], 0, c))
    return pl.pallas_call(
        _moe_gateup_kernel,
        grid_spec=pltpu.PrefetchScalarGridSpec(
            num_scalar_prefetch=2,
            grid=(f // tc, r // tg),
            in_specs=[pl.BlockSpec((tg, d), lambda c, i, te, ta: (i, 0)), wspec, wspec],
            out_specs=pl.BlockSpec((tg, tc), lambda c, i, te, ta: (i, c))),
        out_shape=jax.ShapeDtypeStruct((r, f), BF16),
        compiler_params=_params("arbitrary", "arbitrary"),
        name="moe_gateup",
    )(tile_e, tile_on, hs, wg, wu)


def _moe_down_kernel(te_ref, ta_ref, a_ref, w_ref, o_ref):
    i = pl.program_id(1)

    @pl.when(ta_ref[i] > 0)
    def _():
        o_ref[...] = jnp.dot(a_ref[...], w_ref[...].astype(BF16), preferred_element_type=F32)

    @pl.when(ta_ref[i] == 0)
    def _():
        o_ref[...] = jnp.zeros(o_ref.shape, o_ref.dtype)


def _moe_down(tile_e, tile_on, a, wd, layer, tn=512):
    r, f = a.shape
    d = wd.shape[-1]
    tg = MOE_TILE
    return pl.pallas_call(
        _moe_down_kernel,
        grid_spec=pltpu.PrefetchScalarGridSpec(
            num_scalar_prefetch=2,
            grid=(d // tn, r // tg),
            in_specs=[pl.BlockSpec((tg, f), lambda j, i, te, ta: (i, 0)),
                      pl.BlockSpec((None, None, f, tn), lambda j, i, te, ta: (layer, te[i], 0, j))],
            out_specs=pl.BlockSpec((tg, tn), lambda j, i, te, ta: (i, j))),
        out_shape=jax.ShapeDtypeStruct((r, d), F32),
        compiler_params=_params("arbitrary", "arbitrary"),
        name="moe_down",
    )(tile_e, tile_on, a, wd)


def _router_kernel(h_ref, wr_ref, br_ref, o_ref, *, ne):
    logits = jnp.dot(h_ref[...], wr_ref[...].astype(BF16), preferred_element_type=F32) + br_ref[...]
    lane = lax.broadcasted_iota(jnp.int32, logits.shape, 1).astype(F32)
    lg = jnp.where(lane < ne, logits, -jnp.inf)
    v1 = jnp.max(lg, axis=1, keepdims=True)
    i1 = jnp.min(jnp.where(lg == v1, lane, float(LANES)), axis=1, keepdims=True)
    lg2 = jnp.where(lane == i1, -jnp.inf, lg)
    v2 = jnp.max(lg2, axis=1, keepdims=True)
    i2 = jnp.min(jnp.where(lg2 == v2, lane, float(LANES)), axis=1, keepdims=True)
    e2 = jnp.exp(v2 - v1)
    w1 = 1.0 / (1.0 + e2)
    w2 = e2 / (1.0 + e2)
    o_ref[...] = jnp.where(lane == 0, i1, jnp.where(lane == 1, i2,
                           jnp.where(lane == 2, w1, jnp.where(lane == 3, w2, 0.0))))


def _router(h, w_r, b_r, tm=1024):
    m, d = h.shape
    ne = w_r.shape[1]
    tm = min(tm, m)
    wr = jnp.zeros((d, LANES), F32).at[:, :ne].set(w_r)
    br = jnp.zeros((1, LANES), F32).at[0, :ne].set(b_r)
    return pl.pallas_call(
        functools.partial(_router_kernel, ne=ne),
        grid=(m // tm,),
        in_specs=[pl.BlockSpec((tm, d), lambda i: (i, 0)),
                  pl.BlockSpec((d, LANES), lambda i: (0, 0)),
                  pl.BlockSpec((1, LANES), lambda i: (0, 0))],
        out_specs=pl.BlockSpec((tm, LANES), lambda i: (i, 0)),
        out_shape=jax.ShapeDtypeStruct((m, LANES), F32),
        compiler_params=_params("arbitrary"),
        name="moe_router",
    )(h, wr, br)


def _row_copy(src_hbm, row, buf, slot, sem):
    return pltpu.make_async_copy(src_hbm.at[pl.ds(row, 1), :], buf.at[pl.ds(slot, 1), :], sem)


def _dispatch_kernel(src_ref, nrows_ref, x_hbm, g_ref, o_ref, buf, sem, *, rows):
    base = pl.program_id(0) * rows
    active = base < nrows_ref[0]

    @pl.when(active)
    def _():
        def start(c, carry):
            for u in range(DMA_UNROLL):
                r = u * (rows // DMA_UNROLL) + c
                _row_copy(x_hbm, src_ref[base + r], buf, r, sem).start(priority=u % 2)
            return carry

        def wait(c, carry):
            for u in range(DMA_UNROLL):
                _row_copy(x_hbm, 0, buf, c * DMA_UNROLL + u, sem).wait()
            return carry

        lax.fori_loop(0, rows // DMA_UNROLL, start, 0)
        lax.fori_loop(0, rows // DMA_UNROLL, wait, 0)
        x = buf[...]
        ms = jnp.mean(x * x, axis=-1, keepdims=True)
        o_ref[...] = (x * lax.rsqrt(ms + EPS) * g_ref[...]).astype(o_ref.dtype)

    @pl.when(jnp.logical_not(active))
    def _():
        o_ref[...] = jnp.zeros(o_ref.shape, o_ref.dtype)


def _dispatch(src, n_used, x, g, n_rows):
    m, d = x.shape
    rows = 2 * GATHER_ROWS
    return pl.pallas_call(
        functools.partial(_dispatch_kernel, rows=rows),
        grid_spec=pltpu.PrefetchScalarGridSpec(
            num_scalar_prefetch=2,
            grid=(n_rows // rows,),
            in_specs=[pl.BlockSpec(memory_space=pl.ANY),
                      pl.BlockSpec((1, d), lambda i, s, n: (0, 0))],
            out_specs=pl.BlockSpec((rows, d), lambda i, s, n: (i, 0)),
            scratch_shapes=[pltpu.VMEM((rows, d), F32), pltpu.SemaphoreType.DMA(())]),
        out_shape=jax.ShapeDtypeStruct((n_rows, d), BF16),
        compiler_params=_params("arbitrary"),
        name="moe_dispatch",
    )(src, n_used, x, g.reshape(1, d))


def _combine_kernel(p1_ref, p2_ref, x_ref, route_ref, g_ref, y_hbm, *rest, rows, emit_x):
    if emit_x:
        o_ref, hn_ref, buf1, buf2, sem = rest
    else:
        hn_ref, buf1, buf2, sem = rest
    base = pl.program_id(0) * rows

    def start(c, carry):
        for u in range(DMA_UNROLL):
            r = c * DMA_UNROLL + u
            _row_copy(y_hbm, p1_ref[base + r], buf1, r, sem.at[0]).start(priority=0)
            _row_copy(y_hbm, p2_ref[base + r], buf2, r, sem.at[1]).start(priority=1)
        return carry

    def wait(c, carry):
        for u in range(DMA_UNROLL):
            r = c * DMA_UNROLL + u
            _row_copy(y_hbm, 0, buf1, r, sem.at[0]).wait()
            _row_copy(y_hbm, 0, buf2, r, sem.at[1]).wait()
        return carry

    lax.fori_loop(0, rows // DMA_UNROLL, start, 0)
    lax.fori_loop(0, rows // DMA_UNROLL, wait, 0)
    route = route_ref[...]
    out = x_ref[...] + route[:, 2:3] * buf1[...] + route[:, 3:4] * buf2[...]
    if emit_x:
        o_ref[...] = out
    ms = jnp.mean(out * out, axis=-1, keepdims=True)
    hn_ref[...] = (out * lax.rsqrt(ms + EPS) * g_ref[...]).astype(hn_ref.dtype)


def _combine(p1, p2, x, route, y, next_gain, next_dtype, emit_x):
    m, d = x.shape
    rows = GATHER_ROWS
    tile = pl.BlockSpec((rows, d), lambda i, a, b: (i, 0))
    out_specs = [tile, tile] if emit_x else [tile]
    out_shape = [jax.ShapeDtypeStruct((m, d), next_dtype)]
    if emit_x:
        out_shape.insert(0, jax.ShapeDtypeStruct((m, d), F32))
    res = pl.pallas_call(
        functools.partial(_combine_kernel, rows=rows, emit_x=emit_x),
        grid_spec=pltpu.PrefetchScalarGridSpec(
            num_scalar_prefetch=2,
            grid=(m // rows,),
            in_specs=[tile,
                      pl.BlockSpec((rows, LANES), lambda i, a, b: (i, 0)),
                      pl.BlockSpec((1, d), lambda i, a, b: (0, 0)),
                      pl.BlockSpec(memory_space=pl.ANY)],
            out_specs=out_specs,
            scratch_shapes=[pltpu.VMEM((rows, d), F32), pltpu.VMEM((rows, d), F32),
                            pltpu.SemaphoreType.DMA((2,))]),
        out_shape=out_shape,
        compiler_params=_params("arbitrary"),
        name="moe_combine",
    )(p1, p2, x, route, next_gain.reshape(1, d), y)
    return (res[0], res[1]) if emit_x else (None, res[0])


def _moe_plan(route, m, ne, tg):
    i1 = route[:, 0].astype(jnp.int32)
    i2 = route[:, 1].astype(jnp.int32)
    experts = jnp.arange(ne, dtype=jnp.int32)
    sel = ((i1[:, None] == experts) | (i2[:, None] == experts)).astype(jnp.int32)
    counts = jnp.sum(sel, axis=0)
    padded = ((counts + tg - 1) // tg) * tg
    ends = jnp.cumsum(padded)
    pos = (ends - padded)[None, :] + jnp.cumsum(sel, axis=0) - sel
    p1 = jnp.take_along_axis(pos, i1[:, None], axis=1)[:, 0]
    p2 = jnp.take_along_axis(pos, i2[:, None], axis=1)[:, 0]
    n_tiles = (2 * m) // tg + ne
    tok = jnp.arange(m, dtype=jnp.int32)
    filler = jnp.arange(n_tiles * tg, dtype=jnp.int32) % m
    src = filler.at[jnp.concatenate([p1, p2])].set(jnp.concatenate([tok, tok]))
    tile_start = jnp.arange(n_tiles, dtype=jnp.int32) * tg
    tile_e = jnp.minimum(jnp.sum((tile_start[:, None] >= ends[None, :]).astype(jnp.int32), axis=1), ne - 1)
    tile_on = (tile_start < ends[-1]).astype(jnp.int32)
    return p1, p2, src, ends[-1:], tile_e, tile_on, n_tiles * tg


def _moe_block(x, h, g_ffn, w_r, b_r, wg, wu, wd, layer, next_gain, next_dtype, emit_x):
    m, d = x.shape
    route = _router(h, w_r, b_r)
    p1, p2, src, n_used, tile_e, tile_on, n_rows = _moe_plan(route, m, N_EXPERTS, MOE_TILE)
    hs = _dispatch(src, n_used, x, g_ffn, n_rows)
    a = _moe_gateup(tile_e, tile_on, hs, wg, wu, layer)
    y = _moe_down(tile_e, tile_on, a, wd, layer)
    return _combine(p1, p2, x, route, y, next_gain, next_dtype, emit_x)


def kernel(x, mem, norm_mix, w_in, conv_dw_w, conv_dw_b, conv_ln_g, conv_ln_b, qk_conv_w, qk_conv_b, b_igate, b_fgate, mlstm_norm_g, w_out, norm_cross, norm_mem, w_cq, w_ck, w_cv, w_co, norm_ffn, w_gate_dense, w_up_dense, w_down_dense, w_router, b_router, w_gate_moe, w_up_moe, w_down_moe, norm_final):
    batch, seq, d = x.shape
    mem_len = mem.shape[1]
    depth = norm_mix.shape[0]
    d_conv = conv_dw_w.shape[2]
    d_mlstm = mlstm_norm_g.shape[1]
    n_main = 2 * d_conv + 4 * d_mlstm
    nh = N_MLSTM_HEADS
    xf = x.reshape(batch * seq, d)
    memf = mem.reshape(batch * mem_len, d)

    w_in_nk = jnp.swapaxes(w_in, 1, 2)
    h = _rmsnorm(xf, norm_mix[0], BF16)
    for l in range(depth):
        last = l == depth - 1
        next_gain = norm_final if last else norm_mix[l + 1]
        next_dtype = F32 if last else BF16
        z = _matmul([h], w_in_nk, (l,), n_cols=n_main, w_is_nk=True, tn=1024, name="w_in")
        zgt = _gates(h, w_in_nk, l, n_main, 2 * nh)
        yc = _conformer(z, conv_dw_w[l], conv_dw_b[l], conv_ln_g[l], conv_ln_b[l], batch, seq)
        ym = _mlstm(z, zgt.T, zgt, qk_conv_w[l], qk_conv_b[l], b_igate[l], b_fgate[l], mlstm_norm_g[l],
                    batch, seq)
        xf, h = _matmul([yc, ym], w_out, (l,), residual=xf, norm_gain=norm_cross[l], tm=512, tn=d,
                        name="w_out")
        mem_n = _rmsnorm(memf, norm_mem[l], BF16)
        q = _matmul([h], w_cq, (l,), out_dtype=BF16, tn=1024, name="w_cq")
        kk = _matmul([mem_n], w_ck, (l,), out_dtype=BF16, name="w_ck")
        vv = _matmul([mem_n], w_cv, (l,), out_dtype=BF16, name="w_cv")
        att = _xattn(q, kk, vv, batch, seq, mem_len)
        xf, h = _matmul([att], w_co, (l,), residual=xf, norm_gain=norm_ffn[l], tm=512, tn=d, name="w_co")
        j = l // 2
        if l % 2 == 0:
            a = _gateup(h, w_gate_dense, w_up_dense, (j,))
            xf = _matmul([a], w_down_dense, (j,), residual=xf, tm=512, tn=512, name="w_down")
            h = _rmsnorm(xf, next_gain, next_dtype)
        else:
            xf, h = _moe_block(xf, h, norm_ffn[l], w_router[j], b_router[j], w_gate_moe, w_up_moe,
                               w_down_moe, j, next_gain, next_dtype, emit_x=not last)
    return h.reshape(batch, seq, d)
```
